```python
import math
import jax, jax.numpy as jnp
from jax import lax
import numpy as np

D_MODEL = 2048
BATCH = 2
SEQ = 8192
DEPTH = 1
DEC_BATCH = 8
DEC_SEQ = 64
PAST_LEN = 2048

CHUNK = 64
N_META = 16
W_A = 1024
CONV_A = 3
W_B = 1024
SSM_H = 16
SSM_G = W_B // SSM_H
SSM_P = 64
D_FF = 5632
CONV_F = 3
EPS = 1e-6
N_IN = 3 * W_A + W_B + 2 * D_MODEL
SPLITS = (W_A, 2 * W_A, 3 * W_A, 3 * W_A + W_B, 3 * W_A + W_B + D_MODEL)

kernel_name = "hybrid_shortconv_s5_convffn_stream_step"


def _rmsnorm(x, g):
    xf = x.astype(jnp.float32)
    y = xf * lax.rsqrt(jnp.mean(xf * xf, axis=-1, keepdims=True) + EPS)
    return (y * g.astype(jnp.float32)).astype(x.dtype)


def _causal_dwconv(v, buf, w):
    k = w.shape[0]
    t = v.shape[1]
    full = jnp.concatenate([buf.astype(v.dtype), v], axis=1)
    out = full[:, 0:t] * w[0]
    for i in range(1, k):
        out = out + full[:, i:i + t] * w[i]
    return out, full[:, t:]


def _linrec_combine(left, right):
    a_l, b_l = left
    a_r, b_r = right
    return a_l * a_r, a_r * b_l + b_r


def _ssm_chunk(h, u_c, abar, bbar, c, d):
    bu = jnp.einsum('blgh,gph->blgp', u_c.astype(jnp.complex64), bbar)
    bu = bu.at[:, 0].add(abar * h)
    a = jnp.broadcast_to(abar, bu.shape)
    _, hs = lax.associative_scan(_linrec_combine, (a, bu), axis=1)
    y = jnp.real(jnp.einsum('blgp,ghp->blgh', hs, c)) + d * u_c
    return hs[:, -1], y


def _ssm_mixer(u, h_re, h_im, lead, lam_re, lam_im, log_dt, b_re, b_im, c_re, c_im, d):
    f32 = jnp.float32
    lam = lax.complex(lam_re.astype(f32), lam_im.astype(f32))
    dt = jnp.exp(log_dt.astype(f32))[:, None]
    abar = jnp.exp(lam * dt)
    bmat = lax.complex(b_re.astype(f32), b_im.astype(f32))
    bbar = ((abar - 1.0) / lam)[..., None] * bmat
    cmat = lax.complex(c_re.astype(f32), c_im.astype(f32))
    dvec = d.astype(f32)
    uf = u.astype(f32)
    h = lax.complex(h_re.astype(f32), h_im.astype(f32))
    ys = []
    if lead > 0:
        h, y0 = _ssm_chunk(h, uf[:, :lead], abar, bbar, cmat, dvec)
        ys.append(y0)
    rest = uf[:, lead:]
    bsz, t = rest.shape[0], rest.shape[1]
    blk = min(CHUNK, t)
    n = t // blk
    chunks = rest.reshape(bsz, n, blk, SSM_G, SSM_H).swapaxes(0, 1)
    h, yc = lax.scan(lambda hh, uc: _ssm_chunk(hh, uc, abar, bbar, cmat, dvec), h, chunks)
    ys.append(yc.swapaxes(0, 1).reshape(bsz, t, SSM_G, SSM_H))
    y = jnp.concatenate(ys, axis=1) if len(ys) > 1 else ys[0]
    return y, jnp.real(h), jnp.imag(h)


def _layer(x, conv_buf, h_re, h_im, ffn_buf, lead, norm_mix_g, w_in, conv_a_w,
           lam_re, lam_im, log_dt, b_re, b_im, c_re, c_im, d, glu_w, glu_b,
           proj_a, proj_b, w_out, norm_ffn_g, w_up, ffn_conv_w, ffn_conv_b, w_down):
    bsz, t, _ = x.shape
    hn = _rmsnorm(x, norm_mix_g)
    z = hn @ w_in
    b_a, c_a, h_a, u_b, gate_a, gate_b = jnp.split(z, SPLITS, axis=-1)
    conv_out, new_conv_buf = _causal_dwconv(c_a * h_a, conv_buf, conv_a_w)
    out_a = b_a * conv_out
    y_b, new_re, new_im = _ssm_mixer(u_b.reshape(bsz, t, SSM_G, SSM_H), h_re, h_im, lead,
                                     lam_re, lam_im, log_dt, b_re, b_im, c_re, c_im, d)
    y_b = jax.nn.gelu(y_b.reshape(bsz, t, W_B)).astype(x.dtype)
    out_b = y_b * jax.nn.sigmoid(y_b @ glu_w + glu_b)
    merged = jax.nn.sigmoid(gate_a) * (out_a @ proj_a) + jax.nn.sigmoid(gate_b) * (out_b @ proj_b)
    x = x + merged @ w_out
    up = _rmsnorm(x, norm_ffn_g) @ w_up
    up, new_ffn_buf = _causal_dwconv(up, ffn_buf, ffn_conv_w)
    up = up + ffn_conv_b
    g, v = jnp.split(up, 2, axis=-1)
    x = x + (jax.nn.silu(g) * v) @ w_down
    return x, new_conv_buf, new_re, new_im, new_ffn_buf


def setup_inputs(seed: int = 0) -> dict:
    key = jax.random.key(seed)
    ks = jax.random.split(key, 32)
    f32 = jnp.float32
    nrm = lambda k, s, sc: jax.random.normal(k, s, f32) * sc
    lam_im = jnp.broadcast_to(math.pi * jnp.arange(SSM_P, dtype=f32), (DEPTH, SSM_G, SSM_P))
    return {
        "x_prompt": nrm(ks[0], (BATCH, SEQ, D_MODEL), 1.0),
        "x_sample": nrm(ks[1], (DEC_BATCH, DEC_SEQ, D_MODEL), 1.0),
        "cache_conv_a": nrm(ks[2], (DEPTH, DEC_BATCH, CONV_A - 1, W_A), 1.0),
        "state_ssm_re": nrm(ks[3], (DEPTH, DEC_BATCH, SSM_G, SSM_P), 0.1),
        "state_ssm_im": nrm(ks[4], (DEPTH, DEC_BATCH, SSM_G, SSM_P), 0.1),
        "cache_ffn_conv": nrm(ks[5], (DEPTH, DEC_BATCH, CONV_F - 1, 2 * D_FF), 1.0),
        "meta_tokens": nrm(ks[6], (N_META, D_MODEL), 1.0),
        "norm_mix_g": 1.0 + nrm(ks[7], (DEPTH, D_MODEL), 0.02),
        "w_in": nrm(ks[8], (DEPTH, D_MODEL, N_IN), D_MODEL ** -0.5),
        "conv_a_w": nrm(ks[9], (DEPTH, CONV_A, W_A), CONV_A ** -0.5),
        "ssm_lambda_re": -0.5 + nrm(ks[10], (DEPTH, SSM_G, SSM_P), 0.01),
        "ssm_lambda_im": lam_im + nrm(ks[11], (DEPTH, SSM_G, SSM_P), 0.01),
        "ssm_log_dt": jax.random.uniform(ks[12], (DEPTH, SSM_G), f32, math.log(1e-3), math.log(1e-1)),
        "ssm_b_re": nrm(ks[13], (DEPTH, SSM_G, SSM_P, SSM_H), (2 * SSM_H) ** -0.5),
        "ssm_b_im": nrm(ks[14], (DEPTH, SSM_G, SSM_P, SSM_H), (2 * SSM_H) ** -0.5),
        "ssm_c_re": nrm(ks[15], (DEPTH, SSM_G, SSM_H, SSM_P), (2 * SSM_P) ** -0.5),
        "ssm_c_im": nrm(ks[16], (DEPTH, SSM_G, SSM_H, SSM_P), (2 * SSM_P) ** -0.5),
        "ssm_d": nrm(ks[17], (DEPTH, SSM_G, SSM_H), 1.0),
        "glu_w": nrm(ks[18], (DEPTH, W_B, W_B), W_B ** -0.5),
        "glu_b": nrm(ks[19], (DEPTH, W_B), 0.02),
        "proj_a": nrm(ks[20], (DEPTH, W_A, D_MODEL), W_A ** -0.5),
        "proj_b": nrm(ks[21], (DEPTH, W_B, D_MODEL), W_B ** -0.5),
        "w_out": nrm(ks[22], (DEPTH, D_MODEL, D_MODEL), D_MODEL ** -0.5),
        "norm_ffn_g": 1.0 + nrm(ks[23], (DEPTH, D_MODEL), 0.02),
        "w_up": nrm(ks[24], (DEPTH, D_MODEL, 2 * D_FF), D_MODEL ** -0.5),
        "ffn_conv_w": nrm(ks[25], (DEPTH, CONV_F, 2 * D_FF), CONV_F ** -0.5),
        "ffn_conv_b": nrm(ks[26], (DEPTH, 2 * D_FF), 0.02),
        "w_down": nrm(ks[27], (DEPTH, D_FF, D_MODEL), D_FF ** -0.5),
        "norm_final_g": 1.0 + nrm(ks[28], (D_MODEL,), 0.02),
    }


def reference(x_prompt, x_sample, cache_conv_a, state_ssm_re, state_ssm_im, cache_ffn_conv,
              meta_tokens, norm_mix_g, w_in, conv_a_w, ssm_lambda_re, ssm_lambda_im, ssm_log_dt,
              ssm_b_re, ssm_b_im, ssm_c_re, ssm_c_im, ssm_d, glu_w, glu_b, proj_a, proj_b,
              w_out, norm_ffn_g, w_up, ffn_conv_w, ffn_conv_b, w_down, norm_final_g):
    bsz = x_prompt.shape[0]
    dt = x_prompt.dtype
    xp = jnp.concatenate([jnp.broadcast_to(meta_tokens.astype(dt), (bsz, N_META, D_MODEL)), x_prompt], axis=1)
    xs = x_sample
    zero_conv = jnp.zeros((bsz, CONV_A - 1, W_A), dt)
    zero_h = jnp.zeros((bsz, SSM_G, SSM_P), jnp.float32)
    zero_ffn = jnp.zeros((bsz, CONV_F - 1, 2 * D_FF), dt)
    p_conv, p_re, p_im, p_ffn = [], [], [], []
    s_conv, s_re, s_im, s_ffn = [], [], [], []
    for l in range(DEPTH):
        w = (norm_mix_g[l], w_in[l], conv_a_w[l], ssm_lambda_re[l], ssm_lambda_im[l], ssm_log_dt[l],
             ssm_b_re[l], ssm_b_im[l], ssm_c_re[l], ssm_c_im[l], ssm_d[l], glu_w[l], glu_b[l],
             proj_a[l], proj_b[l], w_out[l], norm_ffn_g[l], w_up[l], ffn_conv_w[l], ffn_conv_b[l], w_down[l])
        xp, a1, a2, a3, a4 = _layer(xp, zero_conv, zero_h, zero_h, zero_ffn, N_META, *w)
        xs, b1, b2, b3, b4 = _layer(xs, cache_conv_a[l], state_ssm_re[l], state_ssm_im[l], cache_ffn_conv[l], 0, *w)
        p_conv.append(a1); p_re.append(a2); p_im.append(a3); p_ffn.append(a4)
        s_conv.append(b1); s_re.append(b2); s_im.append(b3); s_ffn.append(b4)
    y_prompt = _rmsnorm(xp[:, N_META:], norm_final_g)
    y_sample = _rmsnorm(xs, norm_final_g)
    return (y_prompt, y_sample,
            jnp.stack(p_conv), jnp.stack(p_re), jnp.stack(p_im), jnp.stack(p_ffn),
            jnp.stack(s_conv), jnp.stack(s_re), jnp.stack(s_im), jnp.stack(s_ffn))
```

```python
import functools

import jax
import jax.numpy as jnp
from jax import lax
from jax.experimental import pallas as pl
from jax.experimental.pallas import tpu as pltpu

D_MODEL = 2048
W_A = 1024
W_B = 1024
SSM_H = 16
SSM_G = 64
SSM_P = 64
D_FF = 5632
CHUNK = 64
N_META = 16
EPS = 1e-6

BF16 = jnp.bfloat16
F32 = jnp.float32
HIGHEST = lax.Precision.HIGHEST

CARRY_ROWS = 8
ROW_TILE = 512
FF_TILE = 512
SSM_GROUPS_PER_STEP = 2
SSM_ROWS = 272
VMEM_LIMIT = 56 * 1024 * 1024


def _dot(a, b):
    return jnp.dot(a, b, preferred_element_type=F32)


def _dot_hi(a, b):
    return jnp.dot(a, b, preferred_element_type=F32, precision=HIGHEST)


def _rmsnorm(x, g):
    ms = jnp.mean(x * x, axis=-1, keepdims=True)
    return (x * lax.rsqrt(ms + EPS)) * g


def _params(n_axes):
    return pltpu.CompilerParams(
        dimension_semantics=("arbitrary",) * n_axes, vmem_limit_bytes=VMEM_LIMIT)


def _resident(shape):
    return pl.BlockSpec(shape, lambda *_: (0,) * len(shape), pipeline_mode=pl.Buffered(1))


def _conv_segments(v, w, init_ref, cout_ref, ext_ref, carry_ref, segs, chain_tiles):
    pieces = []
    for k, (s, n) in enumerate(segs):
        base = s + CARRY_ROWS * k
        if chain_tiles:
            @pl.when(pl.program_id(0) % chain_tiles == 0)
            def _():
                carry_ref[...] = init_ref[0]

            prev = carry_ref[...]
        else:
            prev = init_ref[k]
        ext_ref[base:base + CARRY_ROWS, :] = prev
        ext_ref[base + CARRY_ROWS:base + CARRY_ROWS + n, :] = v[s:s + n]
        t2 = ext_ref[base + 6:base + 6 + n, :]
        t1 = ext_ref[base + 7:base + 7 + n, :]
        t0 = ext_ref[base + 8:base + 8 + n, :]
        pieces.append(t2 * w[0:1] + t1 * w[1:2] + t0 * w[2:3])
        last = ext_ref[base + n:base + n + CARRY_ROWS, :]
        cout_ref[k] = last
        if chain_tiles:
            carry_ref[...] = last
    return pieces[0] if len(pieces) == 1 else jnp.concatenate(pieces, axis=0)


def _mixer_a_kernel(x_ref, g_ref, w_ref, cw_ref, init_ref, hn_ref, outa_ref, cout_ref,
                    ext_ref, carry_ref, *, segs, chain_tiles):
    hn = _rmsnorm(x_ref[...], g_ref[...]).astype(BF16)
    hn_ref[...] = hn
    zb = _dot(hn, w_ref[:, 0:W_A])
    zc = _dot(hn, w_ref[:, W_A:2 * W_A])
    zh = _dot(hn, w_ref[:, 2 * W_A:3 * W_A])
    conv = _conv_segments(zc * zh, cw_ref[...], init_ref, cout_ref, ext_ref, carry_ref,
                          segs, chain_tiles)
    outa_ref[...] = (zb * conv).astype(BF16)


def _mixer_a(x, g, w_a, conv_w, init, *, tm, segs, chain_tiles):
    rows = x.shape[0]
    n_tiles = rows // tm
    nseg = len(segs)
    n_out_seq = n_tiles // chain_tiles if chain_tiles else 1
    seq = (lambda i: (i // chain_tiles, 0, 0)) if chain_tiles else (lambda i: (0, 0, 0))
    return pl.pallas_call(
        functools.partial(_mixer_a_kernel, segs=segs, chain_tiles=chain_tiles),
        grid=(n_tiles,),
        in_specs=[
            pl.BlockSpec((tm, D_MODEL), lambda i: (i, 0)),
            _resident((1, D_MODEL)),
            _resident((D_MODEL, 3 * W_A)),
            _resident((3, W_A)),
            pl.BlockSpec((nseg, CARRY_ROWS, W_A), lambda i: (0, 0, 0)),
        ],
        out_specs=[
            pl.BlockSpec((tm, D_MODEL), lambda i: (i, 0)),
            pl.BlockSpec((tm, W_A), lambda i: (i, 0)),
            pl.BlockSpec((nseg, CARRY_ROWS, W_A), seq),
        ],
        out_shape=[
            jax.ShapeDtypeStruct((rows, D_MODEL), BF16),
            jax.ShapeDtypeStruct((rows, W_A), BF16),
            jax.ShapeDtypeStruct((n_out_seq * nseg, CARRY_ROWS, W_A), F32),
        ],
        scratch_shapes=[
            pltpu.VMEM((tm + CARRY_ROWS * nseg, W_A), F32),
            pltpu.VMEM((CARRY_ROWS, W_A), F32),
        ],
        compiler_params=_params(1),
        name="mixer_a",
    )(x, g, w_a, conv_w, init)


UG_COLS = 2 * D_MODEL + W_B


def _gates_u_kernel(hn_ref, w_ref, out_ref):
    hn = hn_ref[...]
    for k in range(UG_COLS // 1024):
        cols = slice(k * 1024, (k + 1) * 1024)
        z = _dot(hn, w_ref[:, cols])
        if k < 2 * D_MODEL // 1024:
            z = jax.nn.sigmoid(z)
        out_ref[:, cols] = z.astype(BF16)


def _gates_u(hn, w_ug, *, tm):
    rows = hn.shape[0]
    return pl.pallas_call(
        _gates_u_kernel,
        grid=(rows // tm,),
        in_specs=[
            pl.BlockSpec((tm, D_MODEL), lambda i: (i, 0)),
            _resident((D_MODEL, UG_COLS)),
        ],
        out_specs=pl.BlockSpec((tm, UG_COLS), lambda i: (i, 0)),
        out_shape=jax.ShapeDtypeStruct((rows, UG_COLS), BF16),
        compiler_params=_params(1),
        name="gates_u",
    )(hn, w_ug)


def _ssm_group(u, lre_c, lim_c, ldt_c, lre_r, lim_r, ldt_r, bta, btb, ctr, cti, ddiag, h0,
               sel_lag, sel_ch, sel_row, t_ref, xs_ref):
    lane128 = lax.broadcasted_iota(jnp.int32, (1, 128), 1)
    sgn_conj = jnp.where(lane128 < SSM_P, 1.0, -1.0).astype(F32)
    sgn_mul = -sgn_conj

    def swap(h):
        return pltpu.roll(h, SSM_P, axis=1)

    def cmul(ar, ai, h):
        return ar * h + (ai * sgn_mul) * swap(h)

    dt_c = jnp.exp(ldt_c)
    k_row = lax.broadcasted_iota(jnp.int32, (1, 128), 1).astype(F32)
    mag = jnp.exp((lre_c * dt_c) * k_row)
    ang = (lim_c * dt_c) * k_row
    apt_re = mag * jnp.cos(ang)
    apt_im = mag * jnp.sin(ang)
    ae_re = _dot_hi(apt_re, sel_lag)
    ae_im = _dot_hi(apt_im, sel_lag)
    cte_re = _dot_hi(ctr, sel_ch)
    cte_im = _dot_hi(cti, sel_ch)
    ca_re = cte_re * ae_re - cte_im * ae_im
    ca_im = cte_re * ae_im + cte_im * ae_re
    a1_re = apt_re[:, 1:2]
    a1_im = apt_im[:, 1:2]
    ca1_re = ca_re * a1_re - ca_im * a1_im
    ca1_im = ca_re * a1_im + ca_im * a1_re

    dt_r = jnp.exp(ldt_r)
    xr = lre_r * dt_r
    xi = lim_r * dt_r
    a_re = jnp.exp(xr) * jnp.cos(xi)
    a_im = jnp.exp(xr) * jnp.sin(xi)
    den = lre_r * lre_r + lim_r * lim_r
    nr = a_re - 1.0
    cr = (nr * lre_r + a_im * lim_r) / den
    ci = (a_im * lre_r - nr * lim_r) / den
    bb_a = bta * cr + btb * ci
    bb_b = btb * cr - bta * ci

    k2 = _dot_hi(bb_a * sgn_conj, jnp.concatenate([ca_re, ca_im], axis=0))
    lane1k = lax.broadcasted_iota(jnp.int32, (SSM_H, CHUNK * SSM_H), 1)
    k2 = k2 + jnp.where(lane1k < SSM_H, _dot_hi(ddiag, sel_ch), 0.0)
    k2z = jnp.concatenate([k2, jnp.zeros_like(k2)], axis=1)
    for r in range(8):
        kr = pltpu.roll(k2z, SSM_H * r, axis=1) if r else k2z
        for q in range(CHUNK // 8):
            m = 8 * q + r
            rows = slice(SSM_H * m, SSM_H * (m + 1))
            if q:
                t_ref[rows, 0:128 * q] = jnp.zeros((SSM_H, 128 * q), BF16)
            t_ref[rows, 128 * q:] = kr[:, 0:CHUNK * SSM_H - 128 * q].astype(BF16)

    e_col = (CHUNK - 1 - lax.broadcasted_iota(jnp.int32, (CHUNK, 1), 0)).astype(F32)
    apm = jnp.exp(e_col * xr)
    apa = e_col * xi
    apx_re = _dot_hi(sel_row, apm * jnp.cos(apa))
    apx_im = _dot_hi(sel_row, apm * jnp.sin(apa))
    tile_rows = lambda b: jnp.broadcast_to(b[None], (CHUNK, SSM_H, 128)).reshape(CHUNK * SSM_H, 128)
    bcat = (apx_re * tile_rows(bb_a) + apx_im * tile_rows(bb_b)).astype(BF16)
    ccat = jnp.concatenate([ca1_re, -ca1_im], axis=0).astype(BF16)

    s = _dot(u, bcat)
    al_re = jnp.exp(CHUNK * xr) * jnp.cos(CHUNK * xi)
    al_im = jnp.exp(CHUNK * xr) * jnp.sin(CHUNK * xi)

    h_meta = s[264:265]
    h_sample = cmul(al_re, al_im, h0) + s[256:264]

    n_p = 256
    row = lax.broadcasted_iota(jnp.int32, (n_p, 128), 0)
    x = s[0:n_p] + jnp.where(row < 2, cmul(al_re, al_im, h_meta), 0.0)
    head = 128
    xs_ref[0:head, :] = jnp.zeros((head, 128), F32)
    ar, ai = al_re, al_im
    for step in range(7):
        sh = 2 * (1 << step)
        xs_ref[head:head + n_p, :] = x
        x = x + cmul(ar, ai, xs_ref[head - sh:head + n_p - sh, :])
        ar, ai = ar * ar - ai * ai, 2.0 * ar * ai
    xs_ref[head:head + n_p, :] = x
    h_in_prompt = xs_ref[head - 2:head + n_p - 2, :] + jnp.where(row < 2, h_meta, 0.0)
    h_in = jnp.concatenate([h_in_prompt, h0, jnp.zeros((SSM_ROWS - n_p - 8, 128), F32)], axis=0)

    y = _dot(u, t_ref[...]) + _dot(h_in.astype(BF16), ccat)
    h_fin = jnp.concatenate([x[n_p - 2:n_p], jnp.zeros((6, 128), F32), h_sample], axis=0)
    return jax.nn.gelu(y).astype(BF16), h_fin


def _ssm_kernel(u_ref, lre_c_ref, lim_c_ref, ldt_c_ref, lre_r_ref, lim_r_ref, ldt_r_ref,
                bta_ref, btb_ref, ctr_ref, cti_ref, dd_ref, h0_ref, y_ref, hfin_ref,
                t_ref, xs_ref):
    kl = CHUNK * SSM_H
    lag_of_lane = lax.broadcasted_iota(jnp.int32, (128, kl), 1) >> 4
    sel_lag = (lax.broadcasted_iota(jnp.int32, (128, kl), 0) == lag_of_lane).astype(F32)
    ch_of_lane = lax.broadcasted_iota(jnp.int32, (SSM_H, kl), 1) & (SSM_H - 1)
    sel_ch = (lax.broadcasted_iota(jnp.int32, (SSM_H, kl), 0) == ch_of_lane).astype(F32)
    lag_of_row = lax.broadcasted_iota(jnp.int32, (kl, CHUNK), 0) >> 4
    sel_row = (lax.broadcasted_iota(jnp.int32, (kl, CHUNK), 1) == lag_of_row).astype(F32)
    for g in range(SSM_GROUPS_PER_STEP):
        y, h_fin = _ssm_group(
            u_ref[g], lre_c_ref[g], lim_c_ref[g], ldt_c_ref[g], lre_r_ref[g], lim_r_ref[g],
            ldt_r_ref[g], bta_ref[g], btb_ref[g], ctr_ref[g], cti_ref[g], dd_ref[g], h0_ref[g],
            sel_lag, sel_ch, sel_row, t_ref.at[g % 2], xs_ref)
        y_ref[g] = y
        hfin_ref[g] = h_fin


def _ssm(u, lre_c, lim_c, ldt_c, lre_r, lim_r, ldt_r, bta, btb, ctr, cti, ddiag, h0):
    gb = SSM_GROUPS_PER_STEP
    per_group = lambda *tail: pl.BlockSpec((gb,) + tail, lambda i: (i,) + (0,) * len(tail))
    return pl.pallas_call(
        _ssm_kernel,
        grid=(SSM_G // gb,),
        in_specs=[
            per_group(SSM_ROWS, CHUNK * SSM_H),
            per_group(SSM_P, 1), per_group(SSM_P, 1), per_group(SSM_P, 1),
            per_group(1, 128), per_group(1, 128), per_group(1, 128),
            per_group(SSM_H, 128), per_group(SSM_H, 128),
            per_group(SSM_P, SSM_H), per_group(SSM_P, SSM_H),
            per_group(SSM_H, SSM_H),
            per_group(8, 128),
        ],
        out_specs=[per_group(SSM_ROWS, CHUNK * SSM_H), per_group(16, 128)],
        out_shape=[
            jax.ShapeDtypeStruct((SSM_G, SSM_ROWS, CHUNK * SSM_H), BF16),
            jax.ShapeDtypeStruct((SSM_G, 16, 128), F32),
        ],
        scratch_shapes=[
            pltpu.VMEM((2, CHUNK * SSM_H, CHUNK * SSM_H), BF16),
            pltpu.VMEM((128 + 256, 128), F32),
        ],
        compiler_params=_params(1),
        name="ssm",
    )(u, lre_c, lim_c, ldt_c, lre_r, lim_r, ldt_r, bta, btb, ctr, cti, ddiag, h0)


def _merge_kernel(yb_ref, outa_ref, sga_ref, sgb_ref, x_ref, gluw_ref, glub_ref, pa_ref,
                  pb_ref, wo_ref, x1_ref):
    yb = yb_ref[...]
    glu = _dot(yb, gluw_ref[...]) + glub_ref[...]
    out_b = (yb.astype(F32) * jax.nn.sigmoid(glu)).astype(BF16)
    merged = (sga_ref[...].astype(F32) * _dot(outa_ref[...], pa_ref[...])
              + sgb_ref[...].astype(F32) * _dot(out_b, pb_ref[...]))
    x1_ref[...] = x_ref[...] + _dot(merged.astype(BF16), wo_ref[...])


def _merge(yb, outa, ug, x, glu_w, glu_b, proj_a, proj_b, w_out, *, tm):
    rows = x.shape[0]
    return pl.pallas_call(
        _merge_kernel,
        grid=(rows // tm,),
        in_specs=[
            pl.BlockSpec((tm, W_B), lambda i: (i, 0)),
            pl.BlockSpec((tm, W_A), lambda i: (i, 0)),
            pl.BlockSpec((tm, D_MODEL), lambda i: (i, 0)),
            pl.BlockSpec((tm, D_MODEL), lambda i: (i, 1)),
            pl.BlockSpec((tm, D_MODEL), lambda i: (i, 0)),
            _resident((W_B, W_B)),
            _resident((1, W_B)),
            _resident((W_A, D_MODEL)),
            _resident((W_B, D_MODEL)),
            _resident((D_MODEL, D_MODEL)),
        ],
        out_specs=pl.BlockSpec((tm, D_MODEL), lambda i: (i, 0)),
        out_shape=jax.ShapeDtypeStruct((rows, D_MODEL), F32),
        compiler_params=_params(1),
        name="merge",
    )(yb, outa, ug, ug, x, glu_w, glu_b, proj_a, proj_b, w_out)


def _ffn_kernel(x1_ref, g_ref, wug_ref, wuv_ref, cwg_ref, cwv_ref, bg_ref, bv_ref, wd_ref,
                gfin_ref, initg_ref, initv_ref, y_ref, coutg_ref, coutv_ref,
                hn_ref, extg_ref, extv_ref, carryg_ref, carryv_ref, *, segs, chain_tiles):
    f = pl.program_id(1)
    n_f = pl.num_programs(1)

    @pl.when(f == 0)
    def _():
        hn_ref[...] = _rmsnorm(x1_ref[...], g_ref[...]).astype(BF16)

    hn = hn_ref[...]
    up_g = _dot(hn, wug_ref[...])
    up_v = _dot(hn, wuv_ref[...])
    conv_g = _conv_segments(up_g, cwg_ref[...], initg_ref, coutg_ref.at[:, f], extg_ref,
                            carryg_ref.at[f], segs, chain_tiles) + bg_ref[...]
    conv_v = _conv_segments(up_v, cwv_ref[...], initv_ref, coutv_ref.at[:, f], extv_ref,
                            carryv_ref.at[f], segs, chain_tiles) + bv_ref[...]
    act = (jax.nn.silu(conv_g) * conv_v).astype(BF16)
    down = _dot(act, wd_ref[...])

    @pl.when(f == 0)
    def _():
        y_ref[...] = down

    @pl.when(f > 0)
    def _():
        y_ref[...] += down

    @pl.when(f == n_f - 1)
    def _():
        y_ref[...] = _rmsnorm(x1_ref[...] + y_ref[...], gfin_ref[...])


def _ffn(x1, g, w_up, conv_w, conv_b, w_down, g_fin, init_g, init_v, *, tm, segs, chain_tiles):
    rows = x1.shape[0]
    n_tiles = rows // tm
    n_f = D_FF // FF_TILE
    nseg = len(segs)
    n_out_seq = n_tiles // chain_tiles if chain_tiles else 1
    seq = (lambda i, f: (i // chain_tiles, 0, 0, 0)) if chain_tiles else (lambda i, f: (0, 0, 0, 0))
    carry_block = (nseg, CARRY_ROWS, FF_TILE)
    cout_block = (nseg, n_f, CARRY_ROWS, FF_TILE)
    cout_shape = jax.ShapeDtypeStruct((n_out_seq * nseg, n_f, CARRY_ROWS, FF_TILE), F32)
    y, cout_g, cout_v = pl.pallas_call(
        functools.partial(_ffn_kernel, segs=segs, chain_tiles=chain_tiles),
        grid=(n_tiles, n_f),
        in_specs=[
            pl.BlockSpec((tm, D_MODEL), lambda i, f: (i, 0)),
            pl.BlockSpec((1, D_MODEL), lambda i, f: (0, 0)),
            pl.BlockSpec((D_MODEL, FF_TILE), lambda i, f: (0, f)),
            pl.BlockSpec((D_MODEL, FF_TILE), lambda i, f: (0, n_f + f)),
            pl.BlockSpec((3, FF_TILE), lambda i, f: (0, f)),
            pl.BlockSpec((3, FF_TILE), lambda i, f: (0, n_f + f)),
            pl.BlockSpec((1, FF_TILE), lambda i, f: (0, f)),
            pl.BlockSpec((1, FF_TILE), lambda i, f: (0, n_f + f)),
            pl.BlockSpec((FF_TILE, D_MODEL), lambda i, f: (f, 0)),
            pl.BlockSpec((1, D_MODEL), lambda i, f: (0, 0)),
            pl.BlockSpec(carry_block, lambda i, f: (0, 0, f)),
            pl.BlockSpec(carry_block, lambda i, f: (0, 0, f)),
        ],
        out_specs=[
            pl.BlockSpec((tm, D_MODEL), lambda i, f: (i, 0)),
            pl.BlockSpec(cout_block, seq),
            pl.BlockSpec(cout_block, seq),
        ],
        out_shape=[jax.ShapeDtypeStruct((rows, D_MODEL), F32), cout_shape, cout_shape],
        scratch_shapes=[
            pltpu.VMEM((tm, D_MODEL), BF16),
            pltpu.VMEM((tm + CARRY_ROWS * nseg, FF_TILE), F32),
            pltpu.VMEM((tm + CARRY_ROWS * nseg, FF_TILE), F32),
            pltpu.VMEM((n_f, CARRY_ROWS, FF_TILE), F32),
            pltpu.VMEM((n_f, CARRY_ROWS, FF_TILE), F32),
        ],
        compiler_params=_params(2),
        name="ffn",
    )(x1, g, w_up, w_up, conv_w, conv_w, conv_b, conv_b, w_down, g_fin, init_g, init_v)
    widen = lambda c: c.transpose(0, 2, 1, 3).reshape(c.shape[0], CARRY_ROWS, D_FF)
    return y, widen(cout_g), widen(cout_v)


def _history_block(buf):
    return jnp.pad(buf, ((0, 0), (CARRY_ROWS - buf.shape[1], 0), (0, 0)))


def kernel(x_prompt, x_sample, cache_conv_a, state_ssm_re, state_ssm_im, cache_ffn_conv,
           meta_tokens, norm_mix_g, w_in, conv_a_w, ssm_lambda_re, ssm_lambda_im, ssm_log_dt,
           ssm_b_re, ssm_b_im, ssm_c_re, ssm_c_im, ssm_d, glu_w, glu_b, proj_a, proj_b,
           w_out, norm_ffn_g, w_up, ffn_conv_w, ffn_conv_b, w_down, norm_final_g):
    batch, seq, _ = x_prompt.shape
    dec_batch, dec_seq, _ = x_sample.shape
    assert dec_seq == CHUNK and seq % CHUNK == 0 and seq % ROW_TILE == 0
    n_chunks = seq // CHUNK
    assert batch * n_chunks == 256 and dec_batch == 8

    w_a = w_in[0, :, :3 * W_A].astype(BF16)
    w_ug = jnp.concatenate([w_in[0, :, 3 * W_A + W_B:], w_in[0, :, 3 * W_A:3 * W_A + W_B]],
                           axis=1).astype(BF16)
    glu_w16, proj_a16, proj_b16 = glu_w[0].astype(BF16), proj_a[0].astype(BF16), proj_b[0].astype(BF16)
    w_out16, w_up16, w_down16 = w_out[0].astype(BF16), w_up[0].astype(BF16), w_down[0].astype(BF16)
    g_mix, g_ffn, g_fin = norm_mix_g[0][None], norm_ffn_g[0][None], norm_final_g[None]
    glu_b2, ffn_b2 = glu_b[0][None], ffn_conv_b[0][None]

    n_s = dec_batch * dec_seq
    rows_s = n_s + N_META
    x_s = jnp.concatenate([x_sample.reshape(n_s, D_MODEL), meta_tokens], axis=0)
    segs_s = tuple((b * dec_seq, dec_seq) for b in range(dec_batch)) + ((n_s, N_META),)
    x_p = x_prompt.reshape(batch * seq, D_MODEL)
    segs_p = ((0, ROW_TILE),)
    chain = seq // ROW_TILE

    zero_hist = lambda c: jnp.zeros((1, CARRY_ROWS, c), F32)
    init_a_s = jnp.concatenate([_history_block(cache_conv_a[0]), zero_hist(W_A)], axis=0)
    hn_s, outa_s, ca_s = _mixer_a(x_s, g_mix, w_a, conv_a_w[0], init_a_s,
                                  tm=rows_s, segs=segs_s, chain_tiles=0)
    hn_p, outa_p, ca_p = _mixer_a(x_p, g_mix, w_a, conv_a_w[0], ca_s[dec_batch:],
                                  tm=ROW_TILE, segs=segs_p, chain_tiles=chain)
    ug_s = _gates_u(hn_s, w_ug, tm=rows_s)
    ug_p = _gates_u(hn_p, w_ug, tm=ROW_TILE)

    u_p = ug_p[:, 2 * D_MODEL:].reshape(batch, n_chunks, CHUNK, SSM_G, SSM_H)
    u_p = u_p.transpose(3, 1, 0, 2, 4).reshape(SSM_G, n_chunks * batch, CHUNK * SSM_H)
    u_s = ug_s[:n_s, 2 * D_MODEL:].reshape(dec_batch, CHUNK, SSM_G, SSM_H)
    u_s = u_s.transpose(2, 0, 1, 3).reshape(SSM_G, dec_batch, CHUNK * SSM_H)
    u_m = ug_s[n_s:, 2 * D_MODEL:].reshape(N_META, SSM_G, SSM_H).transpose(1, 0, 2)
    u_m = jnp.pad(u_m, ((0, 0), (CHUNK - N_META, 0), (0, 0))).reshape(SSM_G, 1, CHUNK * SSM_H)
    pad_rows = SSM_ROWS - n_chunks * batch - dec_batch - 1
    u_all = jnp.concatenate(
        [u_p, u_s, u_m, jnp.zeros((SSM_G, pad_rows, CHUNK * SSM_H), BF16)], axis=1)

    dup = lambda v: jnp.concatenate([v, v], axis=-1)[:, None, :]
    lre, lim = ssm_lambda_re[0], ssm_lambda_im[0]
    ldt = jnp.broadcast_to(ssm_log_dt[0][:, None], (SSM_G, SSM_P))
    bt_re, bt_im = ssm_b_re[0].swapaxes(1, 2), ssm_b_im[0].swapaxes(1, 2)
    bta = jnp.concatenate([bt_re, bt_im], axis=-1)
    btb = jnp.concatenate([-bt_im, bt_re], axis=-1)
    ddiag = ssm_d[0][:, :, None] * jnp.eye(SSM_H, dtype=F32)
    h0 = jnp.concatenate([state_ssm_re[0], state_ssm_im[0]], axis=-1).swapaxes(0, 1)
    y_all, h_fin = _ssm(u_all, lre[:, :, None], lim[:, :, None], ldt[:, :, None],
                        dup(lre), dup(lim), dup(ldt), bta, btb,
                        ssm_c_re[0].swapaxes(1, 2), ssm_c_im[0].swapaxes(1, 2), ddiag, h0)

    n_pc = n_chunks * batch
    yb_p = y_all[:, :n_pc].reshape(SSM_G, n_chunks, batch, CHUNK, SSM_H)
    yb_p = yb_p.transpose(2, 1, 3, 0, 4).reshape(batch * seq, W_B)
    yb_s = y_all[:, n_pc:n_pc + dec_batch].reshape(SSM_G, dec_batch, CHUNK, SSM_H)
    yb_s = yb_s.transpose(1, 2, 0, 3).reshape(n_s, W_B)
    yb_m = y_all[:, n_pc + dec_batch].reshape(SSM_G, CHUNK, SSM_H)[:, CHUNK - N_META:]
    yb_s = jnp.concatenate([yb_s, yb_m.transpose(1, 0, 2).reshape(N_META, W_B)], axis=0)

    x1_s = _merge(yb_s, outa_s, ug_s, x_s, glu_w16, glu_b2, proj_a16, proj_b16, w_out16, tm=rows_s)
    x1_p = _merge(yb_p, outa_p, ug_p, x_p, glu_w16, glu_b2, proj_a16, proj_b16, w_out16, tm=ROW_TILE)

    hist_f = _history_block(cache_ffn_conv[0])
    init_g_s = jnp.concatenate([hist_f[:, :, :D_FF], zero_hist(D_FF)], axis=0)
    init_v_s = jnp.concatenate([hist_f[:, :, D_FF:], zero_hist(D_FF)], axis=0)
    y_s, cg_s, cv_s = _ffn(x1_s, g_ffn, w_up16, ffn_conv_w[0], ffn_b2, w_down16, g_fin,
                           init_g_s, init_v_s, tm=rows_s, segs=segs_s, chain_tiles=0)
    y_p, cg_p, cv_p = _ffn(x1_p, g_ffn, w_up16, ffn_conv_w[0], ffn_b2, w_down16, g_fin,
                           cg_s[dec_batch:], cv_s[dec_batch:], tm=ROW_TILE, segs=segs_p,
                           chain_tiles=chain)

    hist = lambda c: c[:, CARRY_ROWS - 2:][None]
    ffn_hist = lambda cg, cv: hist(jnp.concatenate([cg, cv], axis=-1))
    state = lambda h: (h[:, :, :SSM_P].swapaxes(0, 1)[None], h[:, :, SSM_P:].swapaxes(0, 1)[None])
    p_re, p_im = state(h_fin[:, 0:batch])
    s_re, s_im = state(h_fin[:, 8:8 + dec_batch])
    return (y_p.reshape(batch, seq, D_MODEL), y_s[:n_s].reshape(dec_batch, dec_seq, D_MODEL),
            hist(ca_p), p_re, p_im, ffn_hist(cg_p, cv_p),
            hist(ca_s[:dec_batch]), s_re, s_im, ffn_hist(cg_s[:dec_batch], cv_s[:dec_batch]))
```

```python
import functools

import jax
import jax.numpy as jnp
from jax import lax
from jax.experimental import pallas as pl
from jax.experimental.pallas import tpu as pltpu

D_MODEL = 2048
W_A = 1024
W_B = 1024
SSM_H = 16
SSM_G = 64
SSM_P = 64
D_FF = 5632
CHUNK = 64
N_META = 16
EPS = 1e-6

BF16 = jnp.bfloat16
F32 = jnp.float32
HIGHEST = lax.Precision.HIGHEST

CARRY_ROWS = 8
ROW_TILE = 512
FF_TILE = 512
MXU_COLS = 256
GATE_ROWS = 128
SSM_GROUPS_PER_STEP = 2
SSM_ROWS = 272
VMEM_LIMIT = 56 * 1024 * 1024


def _dot(a, b):
    return jnp.dot(a, b, preferred_element_type=F32)


def _dot_hi(a, b):
    return jnp.dot(a, b, preferred_element_type=F32, precision=HIGHEST)


def _rmsnorm(x, g):
    ms = jnp.mean(x * x, axis=-1, keepdims=True)
    return (x * lax.rsqrt(ms + EPS)) * g


def _params(n_axes):
    return pltpu.CompilerParams(
        dimension_semantics=("arbitrary",) * n_axes, vmem_limit_bytes=VMEM_LIMIT)


def _resident(shape):
    return pl.BlockSpec(shape, lambda *_: (0,) * len(shape), pipeline_mode=pl.Buffered(1))


def _stash_rows(ext_ref, v, segs, cols=slice(None)):
    for k, (s, n) in enumerate(segs):
        base = s + CARRY_ROWS * (k + 1)
        ext_ref[base:base + n, cols] = v[s:s + n]


def _conv_prepare(ext_ref, init_ref, cout_ref, carry_ref, segs, chain_tiles):
    for k, (s, n) in enumerate(segs):
        base = s + CARRY_ROWS * k
        ext_ref[base:base + CARRY_ROWS, :] = carry_ref[...] if chain_tiles else init_ref[k]
        last = ext_ref[base + n:base + n + CARRY_ROWS, :]
        cout_ref[k] = last
        if chain_tiles:
            carry_ref[...] = last


def _conv_piece(ext_ref, w_ref, base, rows, cols=slice(None)):
    r0, r1 = rows
    t2 = ext_ref[base + 6 + r0:base + 6 + r1, cols]
    t1 = ext_ref[base + 7 + r0:base + 7 + r1, cols]
    t0 = ext_ref[base + 8 + r0:base + 8 + r1, cols]
    return t2 * w_ref[0:1, cols] + t1 * w_ref[1:2, cols] + t0 * w_ref[2:3, cols]


def _start_stream(init_ref, carry_ref, tile, chain_tiles):
    if chain_tiles:
        @pl.when(tile % chain_tiles == 0)
        def _():
            carry_ref[...] = init_ref[0]


def _mixer_a_kernel(x_ref, g_ref, w_ref, cw_ref, init_ref, hn_ref, outa_ref, cout_ref,
                    ext_ref, carry_ref, *, segs, chain_tiles):
    _start_stream(init_ref, carry_ref, pl.program_id(0), chain_tiles)
    hn = _rmsnorm(x_ref[...], g_ref[...]).astype(BF16)
    hn_ref[...] = hn
    zb = _dot(hn, w_ref[:, 0:W_A])
    zc = _dot(hn, w_ref[:, W_A:2 * W_A])
    zh = _dot(hn, w_ref[:, 2 * W_A:3 * W_A])
    _stash_rows(ext_ref, zc * zh, segs)
    _conv_prepare(ext_ref, init_ref, cout_ref, carry_ref, segs, chain_tiles)
    for k, (s, n) in enumerate(segs):
        conv = _conv_piece(ext_ref, cw_ref, s + CARRY_ROWS * k, (0, n))
        outa_ref[s:s + n, :] = (zb[s:s + n] * conv).astype(BF16)


def _mixer_a(x, g, w_a, conv_w, init, *, tm, segs, chain_tiles):
    rows = x.shape[0]
    n_tiles = rows // tm
    nseg = len(segs)
    n_out_seq = n_tiles // chain_tiles if chain_tiles else 1
    seq = (lambda i: (i // chain_tiles, 0, 0)) if chain_tiles else (lambda i: (0, 0, 0))
    return pl.pallas_call(
        functools.partial(_mixer_a_kernel, segs=segs, chain_tiles=chain_tiles),
        grid=(n_tiles,),
        in_specs=[
            pl.BlockSpec((tm, D_MODEL), lambda i: (i, 0)),
            _resident((1, D_MODEL)),
            _resident((D_MODEL, 3 * W_A)),
            _resident((3, W_A)),
            pl.BlockSpec((nseg, CARRY_ROWS, W_A), lambda i: (0, 0, 0)),
        ],
        out_specs=[
            pl.BlockSpec((tm, D_MODEL), lambda i: (i, 0)),
            pl.BlockSpec((tm, W_A), lambda i: (i, 0)),
            pl.BlockSpec((nseg, CARRY_ROWS, W_A), seq),
        ],
        out_shape=[
            jax.ShapeDtypeStruct((rows, D_MODEL), BF16),
            jax.ShapeDtypeStruct((rows, W_A), BF16),
            jax.ShapeDtypeStruct((n_out_seq * nseg, CARRY_ROWS, W_A), F32),
        ],
        scratch_shapes=[
            pltpu.VMEM((tm + CARRY_ROWS * nseg, W_A), F32),
            pltpu.VMEM((CARRY_ROWS, W_A), F32),
        ],
        compiler_params=_params(1),
        name="mixer_a",
    )(x, g, w_a, conv_w, init)


UG_COLS = 2 * D_MODEL + W_B


def _gates_u_kernel(hn_ref, w_ref, out_ref):
    hn = hn_ref[...]
    for k in range(UG_COLS // 1024):
        cols = slice(k * 1024, (k + 1) * 1024)
        z = _dot(hn, w_ref[:, cols])
        if k < 2 * D_MODEL // 1024:
            z = jax.nn.sigmoid(z)
        out_ref[:, cols] = z.astype(BF16)


def _gates_u(hn, w_ug, *, tm):
    rows = hn.shape[0]
    return pl.pallas_call(
        _gates_u_kernel,
        grid=(rows // tm,),
        in_specs=[
            pl.BlockSpec((tm, D_MODEL), lambda i: (i, 0)),
            _resident((D_MODEL, UG_COLS)),
        ],
        out_specs=pl.BlockSpec((tm, UG_COLS), lambda i: (i, 0)),
        out_shape=jax.ShapeDtypeStruct((rows, UG_COLS), BF16),
        compiler_params=_params(1),
        name="gates_u",
    )(hn, w_ug)


def _ssm_group(u, lre_c, lim_c, ldt_c, lre_r, lim_r, ldt_r, bta, btb, ctr, cti, ddiag, h0,
               sel_lag, sel_ch, sel_row, t_ref, xs_ref):
    lane128 = lax.broadcasted_iota(jnp.int32, (1, 128), 1)
    sgn_conj = jnp.where(lane128 < SSM_P, 1.0, -1.0).astype(F32)
    sgn_mul = -sgn_conj

    def swap(h):
        return pltpu.roll(h, SSM_P, axis=1)

    def cmul(ar, ai, h):
        return ar * h + (ai * sgn_mul) * swap(h)

    dt_c = jnp.exp(ldt_c)
    k_row = lax.broadcasted_iota(jnp.int32, (1, 128), 1).astype(F32)
    mag = jnp.exp((lre_c * dt_c) * k_row)
    ang = (lim_c * dt_c) * k_row
    apt_re = mag * jnp.cos(ang)
    apt_im = mag * jnp.sin(ang)
    ae_re = _dot_hi(apt_re, sel_lag)
    ae_im = _dot_hi(apt_im, sel_lag)
    cte_re = _dot_hi(ctr, sel_ch)
    cte_im = _dot_hi(cti, sel_ch)
    ca_re = cte_re * ae_re - cte_im * ae_im
    ca_im = cte_re * ae_im + cte_im * ae_re
    a1_re = apt_re[:, 1:2]
    a1_im = apt_im[:, 1:2]
    ca1_re = ca_re * a1_re - ca_im * a1_im
    ca1_im = ca_re * a1_im + ca_im * a1_re

    dt_r = jnp.exp(ldt_r)
    xr = lre_r * dt_r
    xi = lim_r * dt_r
    a_re = jnp.exp(xr) * jnp.cos(xi)
    a_im = jnp.exp(xr) * jnp.sin(xi)
    den = lre_r * lre_r + lim_r * lim_r
    nr = a_re - 1.0
    cr = (nr * lre_r + a_im * lim_r) / den
    ci = (a_im * lre_r - nr * lim_r) / den
    bb_a = bta * cr + btb * ci
    bb_b = btb * cr - bta * ci

    k2 = _dot_hi(bb_a * sgn_conj, jnp.concatenate([ca_re, ca_im], axis=0))
    lane1k = lax.broadcasted_iota(jnp.int32, (SSM_H, CHUNK * SSM_H), 1)
    k2 = k2 + jnp.where(lane1k < SSM_H, _dot_hi(ddiag, sel_ch), 0.0)
    k2z = jnp.concatenate([k2, jnp.zeros_like(k2)], axis=1)
    for r in range(8):
        kr = pltpu.roll(k2z, SSM_H * r, axis=1) if r else k2z
        for q in range(CHUNK // 8):
            m = 8 * q + r
            rows = slice(SSM_H * m, SSM_H * (m + 1))
            if q:
                t_ref[rows, 0:128 * q] = jnp.zeros((SSM_H, 128 * q), BF16)
            t_ref[rows, 128 * q:] = kr[:, 0:CHUNK * SSM_H - 128 * q].astype(BF16)

    e_col = (CHUNK - 1 - lax.broadcasted_iota(jnp.int32, (CHUNK, 1), 0)).astype(F32)
    apm = jnp.exp(e_col * xr)
    apa = e_col * xi
    apx_re = _dot_hi(sel_row, apm * jnp.cos(apa))
    apx_im = _dot_hi(sel_row, apm * jnp.sin(apa))
    tile_rows = lambda b: jnp.broadcast_to(b[None], (CHUNK, SSM_H, 128)).reshape(CHUNK * SSM_H, 128)
    bcat = (apx_re * tile_rows(bb_a) + apx_im * tile_rows(bb_b)).astype(BF16)
    ccat = jnp.concatenate([ca1_re, -ca1_im], axis=0).astype(BF16)

    s = _dot(u, bcat)
    al_re = jnp.exp(CHUNK * xr) * jnp.cos(CHUNK * xi)
    al_im = jnp.exp(CHUNK * xr) * jnp.sin(CHUNK * xi)

    h_meta = s[264:265]
    h_sample = cmul(al_re, al_im, h0) + s[256:264]

    n_p = 256
    row = lax.broadcasted_iota(jnp.int32, (n_p, 128), 0)
    x = s[0:n_p] + jnp.where(row < 2, cmul(al_re, al_im, h_meta), 0.0)
    head = 128
    xs_ref[0:head, :] = jnp.zeros((head, 128), F32)
    ar, ai = al_re, al_im
    for step in range(7):
        sh = 2 * (1 << step)
        xs_ref[head:head + n_p, :] = x
        x = x + cmul(ar, ai, xs_ref[head - sh:head + n_p - sh, :])
        ar, ai = ar * ar - ai * ai, 2.0 * ar * ai
    xs_ref[head:head + n_p, :] = x
    h_in_prompt = xs_ref[head - 2:head + n_p - 2, :] + jnp.where(row < 2, h_meta, 0.0)
    h_in = jnp.concatenate([h_in_prompt, h0, jnp.zeros((SSM_ROWS - n_p - 8, 128), F32)], axis=0)

    y = _dot(u, t_ref[...]) + _dot(h_in.astype(BF16), ccat)
    h_fin = jnp.concatenate([x[n_p - 2:n_p], jnp.zeros((6, 128), F32), h_sample], axis=0)
    return jax.nn.gelu(y).astype(BF16), h_fin


def _ssm_kernel(u_ref, lre_c_ref, lim_c_ref, ldt_c_ref, lre_r_ref, lim_r_ref, ldt_r_ref,
                bta_ref, btb_ref, ctr_ref, cti_ref, dd_ref, h0_ref, y_ref, hfin_ref,
                t_ref, xs_ref):
    kl = CHUNK * SSM_H
    lag_of_lane = lax.broadcasted_iota(jnp.int32, (128, kl), 1) >> 4
    sel_lag = (lax.broadcasted_iota(jnp.int32, (128, kl), 0) == lag_of_lane).astype(F32)
    ch_of_lane = lax.broadcasted_iota(jnp.int32, (SSM_H, kl), 1) & (SSM_H - 1)
    sel_ch = (lax.broadcasted_iota(jnp.int32, (SSM_H, kl), 0) == ch_of_lane).astype(F32)
    lag_of_row = lax.broadcasted_iota(jnp.int32, (kl, CHUNK), 0) >> 4
    sel_row = (lax.broadcasted_iota(jnp.int32, (kl, CHUNK), 1) == lag_of_row).astype(F32)
    for g in range(SSM_GROUPS_PER_STEP):
        y, h_fin = _ssm_group(
            u_ref[g], lre_c_ref[g], lim_c_ref[g], ldt_c_ref[g], lre_r_ref[g], lim_r_ref[g],
            ldt_r_ref[g], bta_ref[g], btb_ref[g], ctr_ref[g], cti_ref[g], dd_ref[g], h0_ref[g],
            sel_lag, sel_ch, sel_row, t_ref.at[g % 2], xs_ref)
        y_ref[g] = y
        hfin_ref[g] = h_fin


def _ssm(u, lre_c, lim_c, ldt_c, lre_r, lim_r, ldt_r, bta, btb, ctr, cti, ddiag, h0):
    gb = SSM_GROUPS_PER_STEP
    per_group = lambda *tail: pl.BlockSpec((gb,) + tail, lambda i: (i,) + (0,) * len(tail))
    return pl.pallas_call(
        _ssm_kernel,
        grid=(SSM_G // gb,),
        in_specs=[
            per_group(SSM_ROWS, CHUNK * SSM_H),
            per_group(SSM_P, 1), per_group(SSM_P, 1), per_group(SSM_P, 1),
            per_group(1, 128), per_group(1, 128), per_group(1, 128),
            per_group(SSM_H, 128), per_group(SSM_H, 128),
            per_group(SSM_P, SSM_H), per_group(SSM_P, SSM_H),
            per_group(SSM_H, SSM_H),
            per_group(8, 128),
        ],
        out_specs=[per_group(SSM_ROWS, CHUNK * SSM_H), per_group(16, 128)],
        out_shape=[
            jax.ShapeDtypeStruct((SSM_G, SSM_ROWS, CHUNK * SSM_H), BF16),
            jax.ShapeDtypeStruct((SSM_G, 16, 128), F32),
        ],
        scratch_shapes=[
            pltpu.VMEM((2, CHUNK * SSM_H, CHUNK * SSM_H), BF16),
            pltpu.VMEM((128 + 256, 128), F32),
        ],
        compiler_params=_params(1),
        name="ssm",
    )(u, lre_c, lim_c, ldt_c, lre_r, lim_r, ldt_r, bta, btb, ctr, cti, ddiag, h0)


def _merge_kernel(yb_ref, outa_ref, sga_ref, sgb_ref, x_ref, gluw_ref, glub_ref, pa_ref,
                  pb_ref, wo_ref, x1_ref):
    yb = yb_ref[...]
    glu = _dot(yb, gluw_ref[...]) + glub_ref[...]
    out_b = (yb.astype(F32) * jax.nn.sigmoid(glu)).astype(BF16)
    merged = (sga_ref[...].astype(F32) * _dot(outa_ref[...], pa_ref[...])
              + sgb_ref[...].astype(F32) * _dot(out_b, pb_ref[...]))
    x1_ref[...] = x_ref[...] + _dot(merged.astype(BF16), wo_ref[...])


def _merge(yb, outa, ug, x, glu_w, glu_b, proj_a, proj_b, w_out, *, tm):
    rows = x.shape[0]
    return pl.pallas_call(
        _merge_kernel,
        grid=(rows // tm,),
        in_specs=[
            pl.BlockSpec((tm, W_B), lambda i: (i, 0)),
            pl.BlockSpec((tm, W_A), lambda i: (i, 0)),
            pl.BlockSpec((tm, D_MODEL), lambda i: (i, 0)),
            pl.BlockSpec((tm, D_MODEL), lambda i: (i, 1)),
            pl.BlockSpec((tm, D_MODEL), lambda i: (i, 0)),
            _resident((W_B, W_B)),
            _resident((1, W_B)),
            _resident((W_A, D_MODEL)),
            _resident((W_B, D_MODEL)),
            _resident((D_MODEL, D_MODEL)),
        ],
        out_specs=pl.BlockSpec((tm, D_MODEL), lambda i: (i, 0)),
        out_shape=jax.ShapeDtypeStruct((rows, D_MODEL), F32),
        compiler_params=_params(1),
        name="merge",
    )(yb, outa, ug, ug, x, glu_w, glu_b, proj_a, proj_b, w_out)


def _ffn_kernel(x1_ref, g_ref, wug_ref, wuv_ref, cwg_ref, cwv_ref, bg_ref, bv_ref, wd_ref,
                gfin_ref, initg_ref, initv_ref, y_ref, coutg_ref, coutv_ref,
                hn_ref, extg0_ref, extg1_ref, extv0_ref, extv1_ref, act0_ref, act1_ref,
                carryg_ref, carryv_ref, *, segs, chain_tiles, n_f, n_items):
    t = pl.program_id(0)
    item1 = jnp.clip(t - 1, 0, n_items - 1)
    item2 = jnp.clip(t - 2, 0, n_items - 1)
    f1, tile1 = item1 % n_f, item1 // n_f
    f2 = item2 % n_f
    draining = t >= 2

    @pl.when(t == 0)
    def _():
        extg1_ref[...] = jnp.zeros(extg1_ref.shape, F32)
        extv1_ref[...] = jnp.zeros(extv1_ref.shape, F32)
        act0_ref[...] = jnp.zeros(act0_ref.shape, BF16)
        y_ref[...] = jnp.zeros(y_ref.shape, F32)

    @pl.when(jnp.minimum(t, n_items - 1) % n_f == 0)
    def _():
        hn_ref[...] = _rmsnorm(x1_ref[...], g_ref[...]).astype(BF16)

    @pl.when(jnp.logical_and(f2 == 0, draining))
    def _():
        y_ref[...] = x1_ref[...]

    _start_stream(initg_ref, carryg_ref.at[f1], tile1, chain_tiles)
    _start_stream(initv_ref, carryv_ref.at[f1], tile1, chain_tiles)

    bufs = ((extg0_ref, extv0_ref, act0_ref), (extg1_ref, extv1_ref, act1_ref))
    for par in (0, 1):
        @pl.when(t % 2 == par)
        def _(par=par):
            up_g, up_v, act_in = bufs[par]
            conv_src_g, conv_src_v, act_out = bufs[1 - par]

            def up_piece(dst_ref, w_ref, cols):
                return lambda: _stash_rows(dst_ref, _dot(hn_ref[...], w_ref[:, cols]), segs, cols)

            def down_piece(cols):
                def run():
                    y_ref[:, cols] += _dot(act_in[...], wd_ref[:, cols])
                return run

            def gate_piece(k, s, rows, cols):
                def run():
                    base = s + CARRY_ROWS * k
                    g = _conv_piece(conv_src_g, cwg_ref, base, rows, cols) + bg_ref[:, cols]
                    v = _conv_piece(conv_src_v, cwv_ref, base, rows, cols) + bv_ref[:, cols]
                    act_out[s + rows[0]:s + rows[1], cols] = (jax.nn.silu(g) * v).astype(BF16)
                return run

            col_slices = lambda width, step: [slice(c, c + step) for c in range(0, width, step)]
            mxu = ([(up_piece(up_g, wug_ref, c), 4) for c in col_slices(FF_TILE, MXU_COLS)]
                   + [(up_piece(up_v, wuv_ref, c), 4) for c in col_slices(FF_TILE, MXU_COLS)]
                   + [(down_piece(c), 1) for c in col_slices(D_MODEL, MXU_COLS)])
            vpu = [gate_piece(k, s, (r, min(r + GATE_ROWS, n)), c)
                   for c in col_slices(FF_TILE, 128)
                   for k, (s, n) in enumerate(segs) for r in range(0, n, GATE_ROWS)]

            _conv_prepare(conv_src_g, initg_ref, coutg_ref.at[:, f1], carryg_ref.at[f1], segs, chain_tiles)
            _conv_prepare(conv_src_v, initv_ref, coutv_ref.at[:, f1], carryv_ref.at[f1], segs, chain_tiles)
            total = sum(weight for _, weight in mxu)
            done, issued = 0, 0
            for run, weight in mxu:
                run()
                done += weight
                while issued < len(vpu) and issued * total < done * len(vpu):
                    vpu[issued]()
                    issued += 1

    @pl.when(jnp.logical_and(f2 == n_f - 1, draining))
    def _():
        y_ref[...] = _rmsnorm(y_ref[...], gfin_ref[...])


def _ffn(x1, g, w_up, conv_w, conv_b, w_down, g_fin, init_g, init_v, *, tm, segs, chain_tiles):
    rows = x1.shape[0]
    n_tiles = rows // tm
    n_f = D_FF // FF_TILE
    n_items = n_tiles * n_f
    nseg = len(segs)
    n_out_seq = n_tiles // chain_tiles if chain_tiles else 1
    item = lambda t, lag: jnp.clip(t - lag, 0, n_items - 1)
    ff = lambda lag: (lambda t: item(t, lag) % n_f)
    tile = lambda lag: (lambda t: item(t, lag) // n_f)
    seq1 = lambda t: tile(1)(t) // chain_tiles if chain_tiles else 0
    carry_block = (nseg, CARRY_ROWS, FF_TILE)
    cout_block = (nseg, n_f, CARRY_ROWS, FF_TILE)
    cout_shape = jax.ShapeDtypeStruct((n_out_seq * nseg, n_f, CARRY_ROWS, FF_TILE), F32)
    ext_shape = (tm + CARRY_ROWS * nseg, FF_TILE)
    y, cout_g, cout_v = pl.pallas_call(
        functools.partial(_ffn_kernel, segs=segs, chain_tiles=chain_tiles, n_f=n_f, n_items=n_items),
        grid=(n_items + 2,),
        in_specs=[
            pl.BlockSpec((tm, D_MODEL), lambda t: (tile(0)(t), 0)),
            pl.BlockSpec((1, D_MODEL), lambda t: (0, 0)),
            pl.BlockSpec((D_MODEL, FF_TILE), lambda t: (0, ff(0)(t))),
            pl.BlockSpec((D_MODEL, FF_TILE), lambda t: (0, n_f + ff(0)(t))),
            pl.BlockSpec((3, FF_TILE), lambda t: (0, ff(1)(t))),
            pl.BlockSpec((3, FF_TILE), lambda t: (0, n_f + ff(1)(t))),
            pl.BlockSpec((1, FF_TILE), lambda t: (0, ff(1)(t))),
            pl.BlockSpec((1, FF_TILE), lambda t: (0, n_f + ff(1)(t))),
            pl.BlockSpec((FF_TILE, D_MODEL), lambda t: (ff(2)(t), 0)),
            pl.BlockSpec((1, D_MODEL), lambda t: (0, 0)),
            pl.BlockSpec(carry_block, lambda t: (0, 0, ff(1)(t))),
            pl.BlockSpec(carry_block, lambda t: (0, 0, ff(1)(t))),
        ],
        out_specs=[
            pl.BlockSpec((tm, D_MODEL), lambda t: (tile(2)(t), 0)),
            pl.BlockSpec(cout_block, lambda t: (seq1(t), 0, 0, 0)),
            pl.BlockSpec(cout_block, lambda t: (seq1(t), 0, 0, 0)),
        ],
        out_shape=[jax.ShapeDtypeStruct((rows, D_MODEL), F32), cout_shape, cout_shape],
        scratch_shapes=[
            pltpu.VMEM((tm, D_MODEL), BF16),
            pltpu.VMEM(ext_shape, F32), pltpu.VMEM(ext_shape, F32),
            pltpu.VMEM(ext_shape, F32), pltpu.VMEM(ext_shape, F32),
            pltpu.VMEM((tm, FF_TILE), BF16), pltpu.VMEM((tm, FF_TILE), BF16),
            pltpu.VMEM((n_f, CARRY_ROWS, FF_TILE), F32),
            pltpu.VMEM((n_f, CARRY_ROWS, FF_TILE), F32),
        ],
        compiler_params=_params(1),
        name="ffn",
    )(x1, g, w_up, w_up, conv_w, conv_w, conv_b, conv_b, w_down, g_fin, init_g, init_v)
    widen = lambda c: c.transpose(0, 2, 1, 3).reshape(c.shape[0], CARRY_ROWS, D_FF)
    return y, widen(cout_g), widen(cout_v)


def _history_block(buf):
    return jnp.pad(buf, ((0, 0), (CARRY_ROWS - buf.shape[1], 0), (0, 0)))


def kernel(x_prompt, x_sample, cache_conv_a, state_ssm_re, state_ssm_im, cache_ffn_conv,
           meta_tokens, norm_mix_g, w_in, conv_a_w, ssm_lambda_re, ssm_lambda_im, ssm_log_dt,
           ssm_b_re, ssm_b_im, ssm_c_re, ssm_c_im, ssm_d, glu_w, glu_b, proj_a, proj_b,
           w_out, norm_ffn_g, w_up, ffn_conv_w, ffn_conv_b, w_down, norm_final_g):
    batch, seq, _ = x_prompt.shape
    dec_batch, dec_seq, _ = x_sample.shape
    assert dec_seq == CHUNK and seq % CHUNK == 0 and seq % ROW_TILE == 0
    n_chunks = seq // CHUNK
    assert batch * n_chunks == 256 and dec_batch == 8

    w_a = w_in[0, :, :3 * W_A].astype(BF16)
    w_ug = jnp.concatenate([w_in[0, :, 3 * W_A + W_B:], w_in[0, :, 3 * W_A:3 * W_A + W_B]],
                           axis=1).astype(BF16)
    glu_w16, proj_a16, proj_b16 = glu_w[0].astype(BF16), proj_a[0].astype(BF16), proj_b[0].astype(BF16)
    w_out16, w_up16, w_down16 = w_out[0].astype(BF16), w_up[0].astype(BF16), w_down[0].astype(BF16)
    g_mix, g_ffn, g_fin = norm_mix_g[0][None], norm_ffn_g[0][None], norm_final_g[None]
    glu_b2, ffn_b2 = glu_b[0][None], ffn_conv_b[0][None]

    n_s = dec_batch * dec_seq
    rows_s = n_s + N_META
    x_s = jnp.concatenate([x_sample.reshape(n_s, D_MODEL), meta_tokens], axis=0)
    segs_s = tuple((b * dec_seq, dec_seq) for b in range(dec_batch)) + ((n_s, N_META),)
    x_p = x_prompt.reshape(batch * seq, D_MODEL)
    segs_p = ((0, ROW_TILE),)
    chain = seq // ROW_TILE

    zero_hist = lambda c: jnp.zeros((1, CARRY_ROWS, c), F32)
    init_a_s = jnp.concatenate([_history_block(cache_conv_a[0]), zero_hist(W_A)], axis=0)
    hn_s, outa_s, ca_s = _mixer_a(x_s, g_mix, w_a, conv_a_w[0], init_a_s,
                                  tm=rows_s, segs=segs_s, chain_tiles=0)
    hn_p, outa_p, ca_p = _mixer_a(x_p, g_mix, w_a, conv_a_w[0], ca_s[dec_batch:],
                                  tm=ROW_TILE, segs=segs_p, chain_tiles=chain)
    ug_s = _gates_u(hn_s, w_ug, tm=rows_s)
    ug_p = _gates_u(hn_p, w_ug, tm=ROW_TILE)

    u_p = ug_p[:, 2 * D_MODEL:].reshape(batch, n_chunks, CHUNK, SSM_G, SSM_H)
    u_p = u_p.transpose(3, 1, 0, 2, 4).reshape(SSM_G, n_chunks * batch, CHUNK * SSM_H)
    u_s = ug_s[:n_s, 2 * D_MODEL:].reshape(dec_batch, CHUNK, SSM_G, SSM_H)
    u_s = u_s.transpose(2, 0, 1, 3).reshape(SSM_G, dec_batch, CHUNK * SSM_H)
    u_m = ug_s[n_s:, 2 * D_MODEL:].reshape(N_META, SSM_G, SSM_H).transpose(1, 0, 2)
    u_m = jnp.pad(u_m, ((0, 0), (CHUNK - N_META, 0), (0, 0))).reshape(SSM_G, 1, CHUNK * SSM_H)
    pad_rows = SSM_ROWS - n_chunks * batch - dec_batch - 1
    u_all = jnp.concatenate(
        [u_p, u_s, u_m, jnp.zeros((SSM_G, pad_rows, CHUNK * SSM_H), BF16)], axis=1)

    dup = lambda v: jnp.concatenate([v, v], axis=-1)[:, None, :]
    lre, lim = ssm_lambda_re[0], ssm_lambda_im[0]
    ldt = jnp.broadcast_to(ssm_log_dt[0][:, None], (SSM_G, SSM_P))
    bt_re, bt_im = ssm_b_re[0].swapaxes(1, 2), ssm_b_im[0].swapaxes(1, 2)
    bta = jnp.concatenate([bt_re, bt_im], axis=-1)
    btb = jnp.concatenate([-bt_im, bt_re], axis=-1)
    ddiag = ssm_d[0][:, :, None] * jnp.eye(SSM_H, dtype=F32)
    h0 = jnp.concatenate([state_ssm_re[0], state_ssm_im[0]], axis=-1).swapaxes(0, 1)
    y_all, h_fin = _ssm(u_all, lre[:, :, None], lim[:, :, None], ldt[:, :, None],
                        dup(lre), dup(lim), dup(ldt), bta, btb,
                        ssm_c_re[0].swapaxes(1, 2), ssm_c_im[0].swapaxes(1, 2), ddiag, h0)

    n_pc = n_chunks * batch
    yb_p = y_all[:, :n_pc].reshape(SSM_G, n_chunks, batch, CHUNK, SSM_H)
    yb_p = yb_p.transpose(2, 1, 3, 0, 4).reshape(batch * seq, W_B)
    yb_s = y_all[:, n_pc:n_pc + dec_batch].reshape(SSM_G, dec_batch, CHUNK, SSM_H)
    yb_s = yb_s.transpose(1, 2, 0, 3).reshape(n_s, W_B)
    yb_m = y_all[:, n_pc + dec_batch].reshape(SSM_G, CHUNK, SSM_H)[:, CHUNK - N_META:]
    yb_s = jnp.concatenate([yb_s, yb_m.transpose(1, 0, 2).reshape(N_META, W_B)], axis=0)

    x1_s = _merge(yb_s, outa_s, ug_s, x_s, glu_w16, glu_b2, proj_a16, proj_b16, w_out16, tm=rows_s)
    x1_p = _merge(yb_p, outa_p, ug_p, x_p, glu_w16, glu_b2, proj_a16, proj_b16, w_out16, tm=ROW_TILE)

    hist_f = _history_block(cache_ffn_conv[0])
    init_g_s = jnp.concatenate([hist_f[:, :, :D_FF], zero_hist(D_FF)], axis=0)
    init_v_s = jnp.concatenate([hist_f[:, :, D_FF:], zero_hist(D_FF)], axis=0)
    y_s, cg_s, cv_s = _ffn(x1_s, g_ffn, w_up16, ffn_conv_w[0], ffn_b2, w_down16, g_fin,
                           init_g_s, init_v_s, tm=rows_s, segs=segs_s, chain_tiles=0)
    y_p, cg_p, cv_p = _ffn(x1_p, g_ffn, w_up16, ffn_conv_w[0], ffn_b2, w_down16, g_fin,
                           cg_s[dec_batch:], cv_s[dec_batch:], tm=ROW_TILE, segs=segs_p,
                           chain_tiles=chain)

    hist = lambda c: c[:, CARRY_ROWS - 2:][None]
    ffn_hist = lambda cg, cv: hist(jnp.concatenate([cg, cv], axis=-1))
    state = lambda h: (h[:, :, :SSM_P].swapaxes(0, 1)[None], h[:, :, SSM_P:].swapaxes(0, 1)[None])
    p_re, p_im = state(h_fin[:, 0:batch])
    s_re, s_im = state(h_fin[:, 8:8 + dec_batch])
    return (y_p.reshape(batch, seq, D_MODEL), y_s[:n_s].reshape(dec_batch, dec_seq, D_MODEL),
            hist(ca_p), p_re, p_im, ffn_hist(cg_p, cv_p),
            hist(ca_s[:dec_batch]), s_re, s_im, ffn_hist(cg_s[:dec_batch], cv_s[:dec_batch]))
```

```python
import functools

import jax
import jax.numpy as jnp
from jax import lax
from jax.experimental import pallas as pl
from jax.experimental.pallas import tpu as pltpu

D_MODEL = 2048
W_A = 1024
W_B = 1024
SSM_H = 16
SSM_G = 64
SSM_P = 64
D_FF = 5632
CHUNK = 64
N_META = 16
EPS = 1e-6

BF16 = jnp.bfloat16
F32 = jnp.float32
HIGHEST = lax.Precision.HIGHEST

CARRY_ROWS = 8
ROW_TILE = 512
FFN_ROW_TILE = 1024
FF_TILE = 256
MXU_COLS = 256
GATE_ROWS = 128
SSM_GROUPS_PER_STEP = 2
SSM_ROWS = 272
VMEM_LIMIT = 56 * 1024 * 1024


def _dot(a, b):
    return jnp.dot(a, b, preferred_element_type=F32)


def _dot_hi(a, b):
    return jnp.dot(a, b, preferred_element_type=F32, precision=HIGHEST)


def _rmsnorm(x, g):
    ms = jnp.mean(x * x, axis=-1, keepdims=True)
    return (x * lax.rsqrt(ms + EPS)) * g


def _params(n_axes):
    return pltpu.CompilerParams(
        dimension_semantics=("arbitrary",) * n_axes, vmem_limit_bytes=VMEM_LIMIT)


def _resident(shape):
    return pl.BlockSpec(shape, lambda *_: (0,) * len(shape), pipeline_mode=pl.Buffered(1))


def _stash_rows(ext_ref, v, segs, cols=slice(None)):
    for k, (s, n) in enumerate(segs):
        base = s + CARRY_ROWS * (k + 1)
        ext_ref[base:base + n, cols] = v[s:s + n]


def _conv_prepare(ext_ref, init_ref, cout_ref, carry_ref, segs, chain_tiles):
    for k, (s, n) in enumerate(segs):
        base = s + CARRY_ROWS * k
        ext_ref[base:base + CARRY_ROWS, :] = carry_ref[...] if chain_tiles else init_ref[k]
        last = ext_ref[base + n:base + n + CARRY_ROWS, :]
        cout_ref[k] = last
        if chain_tiles:
            carry_ref[...] = last


def _conv_piece(ext_ref, w_ref, base, rows, cols=slice(None)):
    r0, r1 = rows
    t2 = ext_ref[base + 6 + r0:base + 6 + r1, cols]
    t1 = ext_ref[base + 7 + r0:base + 7 + r1, cols]
    t0 = ext_ref[base + 8 + r0:base + 8 + r1, cols]
    return t2 * w_ref[0:1, cols] + t1 * w_ref[1:2, cols] + t0 * w_ref[2:3, cols]


def _start_stream(init_ref, carry_ref, tile, chain_tiles):
    if chain_tiles:
        @pl.when(tile % chain_tiles == 0)
        def _():
            carry_ref[...] = init_ref[0]


def _mixer_a_kernel(x_ref, g_ref, w_ref, cw_ref, init_ref, hn_ref, outa_ref, cout_ref,
                    ext_ref, carry_ref, *, segs, chain_tiles):
    _start_stream(init_ref, carry_ref, pl.program_id(0), chain_tiles)
    hn = _rmsnorm(x_ref[...], g_ref[...]).astype(BF16)
    hn_ref[...] = hn
    zb = _dot(hn, w_ref[:, 0:W_A])
    zc = _dot(hn, w_ref[:, W_A:2 * W_A])
    zh = _dot(hn, w_ref[:, 2 * W_A:3 * W_A])
    _stash_rows(ext_ref, zc * zh, segs)
    _conv_prepare(ext_ref, init_ref, cout_ref, carry_ref, segs, chain_tiles)
    for k, (s, n) in enumerate(segs):
        conv = _conv_piece(ext_ref, cw_ref, s + CARRY_ROWS * k, (0, n))
        outa_ref[s:s + n, :] = (zb[s:s + n] * conv).astype(BF16)


def _mixer_a(x, g, w_a, conv_w, init, *, tm, segs, chain_tiles):
    rows = x.shape[0]
    n_tiles = rows // tm
    nseg = len(segs)
    n_out_seq = n_tiles // chain_tiles if chain_tiles else 1
    seq = (lambda i: (i // chain_tiles, 0, 0)) if chain_tiles else (lambda i: (0, 0, 0))
    return pl.pallas_call(
        functools.partial(_mixer_a_kernel, segs=segs, chain_tiles=chain_tiles),
        grid=(n_tiles,),
        in_specs=[
            pl.BlockSpec((tm, D_MODEL), lambda i: (i, 0)),
            _resident((1, D_MODEL)),
            _resident((D_MODEL, 3 * W_A)),
            _resident((3, W_A)),
            pl.BlockSpec((nseg, CARRY_ROWS, W_A), lambda i: (0, 0, 0)),
        ],
        out_specs=[
            pl.BlockSpec((tm, D_MODEL), lambda i: (i, 0)),
            pl.BlockSpec((tm, W_A), lambda i: (i, 0)),
            pl.BlockSpec((nseg, CARRY_ROWS, W_A), seq),
        ],
        out_shape=[
            jax.ShapeDtypeStruct((rows, D_MODEL), BF16),
            jax.ShapeDtypeStruct((rows, W_A), BF16),
            jax.ShapeDtypeStruct((n_out_seq * nseg, CARRY_ROWS, W_A), F32),
        ],
        scratch_shapes=[
            pltpu.VMEM((tm + CARRY_ROWS * nseg, W_A), F32),
            pltpu.VMEM((CARRY_ROWS, W_A), F32),
        ],
        compiler_params=_params(1),
        name="mixer_a",
    )(x, g, w_a, conv_w, init)


UG_COLS = 2 * D_MODEL + W_B


def _gates_u_kernel(hn_ref, w_ref, out_ref):
    hn = hn_ref[...]
    for k in range(UG_COLS // 1024):
        cols = slice(k * 1024, (k + 1) * 1024)
        z = _dot(hn, w_ref[:, cols])
        if k < 2 * D_MODEL // 1024:
            z = jax.nn.sigmoid(z)
        out_ref[:, cols] = z.astype(BF16)


def _gates_u(hn, w_ug, *, tm):
    rows = hn.shape[0]
    return pl.pallas_call(
        _gates_u_kernel,
        grid=(rows // tm,),
        in_specs=[
            pl.BlockSpec((tm, D_MODEL), lambda i: (i, 0)),
            _resident((D_MODEL, UG_COLS)),
        ],
        out_specs=pl.BlockSpec((tm, UG_COLS), lambda i: (i, 0)),
        out_shape=jax.ShapeDtypeStruct((rows, UG_COLS), BF16),
        compiler_params=_params(1),
        name="gates_u",
    )(hn, w_ug)


def _ssm_group(u, lre_c, lim_c, ldt_c, lre_r, lim_r, ldt_r, bta, btb, ctr, cti, ddiag, h0,
               sel_lag, sel_ch, sel_row, t_ref, xs_ref):
    lane128 = lax.broadcasted_iota(jnp.int32, (1, 128), 1)
    sgn_conj = jnp.where(lane128 < SSM_P, 1.0, -1.0).astype(F32)
    sgn_mul = -sgn_conj

    def swap(h):
        return pltpu.roll(h, SSM_P, axis=1)

    def cmul(ar, ai, h):
        return ar * h + (ai * sgn_mul) * swap(h)

    dt_c = jnp.exp(ldt_c)
    k_row = lax.broadcasted_iota(jnp.int32, (1, 128), 1).astype(F32)
    mag = jnp.exp((lre_c * dt_c) * k_row)
    ang = (lim_c * dt_c) * k_row
    apt_re = mag * jnp.cos(ang)
    apt_im = mag * jnp.sin(ang)
    ae_re = _dot_hi(apt_re, sel_lag)
    ae_im = _dot_hi(apt_im, sel_lag)
    cte_re = _dot_hi(ctr, sel_ch)
    cte_im = _dot_hi(cti, sel_ch)
    ca_re = cte_re * ae_re - cte_im * ae_im
    ca_im = cte_re * ae_im + cte_im * ae_re
    a1_re = apt_re[:, 1:2]
    a1_im = apt_im[:, 1:2]
    ca1_re = ca_re * a1_re - ca_im * a1_im
    ca1_im = ca_re * a1_im + ca_im * a1_re

    dt_r = jnp.exp(ldt_r)
    xr = lre_r * dt_r
    xi = lim_r * dt_r
    a_re = jnp.exp(xr) * jnp.cos(xi)
    a_im = jnp.exp(xr) * jnp.sin(xi)
    den = lre_r * lre_r + lim_r * lim_r
    nr = a_re - 1.0
    cr = (nr * lre_r + a_im * lim_r) / den
    ci = (a_im * lre_r - nr * lim_r) / den
    bb_a = bta * cr + btb * ci
    bb_b = btb * cr - bta * ci

    k2 = _dot_hi(bb_a * sgn_conj, jnp.concatenate([ca_re, ca_im], axis=0))
    lane1k = lax.broadcasted_iota(jnp.int32, (SSM_H, CHUNK * SSM_H), 1)
    k2 = k2 + jnp.where(lane1k < SSM_H, _dot_hi(ddiag, sel_ch), 0.0)
    k2z = jnp.concatenate([k2, jnp.zeros_like(k2)], axis=1)
    for r in range(8):
        kr = pltpu.roll(k2z, SSM_H * r, axis=1) if r else k2z
        for q in range(CHUNK // 8):
            m = 8 * q + r
            rows = slice(SSM_H * m, SSM_H * (m + 1))
            if q:
                t_ref[rows, 0:128 * q] = jnp.zeros((SSM_H, 128 * q), BF16)
            t_ref[rows, 128 * q:] = kr[:, 0:CHUNK * SSM_H - 128 * q].astype(BF16)

    e_col = (CHUNK - 1 - lax.broadcasted_iota(jnp.int32, (CHUNK, 1), 0)).astype(F32)
    apm = jnp.exp(e_col * xr)
    apa = e_col * xi
    apx_re = _dot_hi(sel_row, apm * jnp.cos(apa))
    apx_im = _dot_hi(sel_row, apm * jnp.sin(apa))
    tile_rows = lambda b: jnp.broadcast_to(b[None], (CHUNK, SSM_H, 128)).reshape(CHUNK * SSM_H, 128)
    bcat = (apx_re * tile_rows(bb_a) + apx_im * tile_rows(bb_b)).astype(BF16)
    ccat = jnp.concatenate([ca1_re, -ca1_im], axis=0).astype(BF16)

    s = _dot(u, bcat)
    al_re = jnp.exp(CHUNK * xr) * jnp.cos(CHUNK * xi)
    al_im = jnp.exp(CHUNK * xr) * jnp.sin(CHUNK * xi)

    h_meta = s[264:265]
    h_sample = cmul(al_re, al_im, h0) + s[256:264]

    n_p = 256
    row = lax.broadcasted_iota(jnp.int32, (n_p, 128), 0)
    x = s[0:n_p] + jnp.where(row < 2, cmul(al_re, al_im, h_meta), 0.0)
    head = 128
    xs_ref[0:head, :] = jnp.zeros((head, 128), F32)
    ar, ai = al_re, al_im
    for step in range(7):
        sh = 2 * (1 << step)
        xs_ref[head:head + n_p, :] = x
        x = x + cmul(ar, ai, xs_ref[head - sh:head + n_p - sh, :])
        ar, ai = ar * ar - ai * ai, 2.0 * ar * ai
    xs_ref[head:head + n_p, :] = x
    h_in_prompt = xs_ref[head - 2:head + n_p - 2, :] + jnp.where(row < 2, h_meta, 0.0)
    h_in = jnp.concatenate([h_in_prompt, h0, jnp.zeros((SSM_ROWS - n_p - 8, 128), F32)], axis=0)

    y = _dot(u, t_ref[...]) + _dot(h_in.astype(BF16), ccat)
    h_fin = jnp.concatenate([x[n_p - 2:n_p], jnp.zeros((6, 128), F32), h_sample], axis=0)
    return jax.nn.gelu(y).astype(BF16), h_fin


def _ssm_kernel(u_ref, lre_c_ref, lim_c_ref, ldt_c_ref, lre_r_ref, lim_r_ref, ldt_r_ref,
                bta_ref, btb_ref, ctr_ref, cti_ref, dd_ref, h0_ref, y_ref, hfin_ref,
                t_ref, xs_ref):
    kl = CHUNK * SSM_H
    lag_of_lane = lax.broadcasted_iota(jnp.int32, (128, kl), 1) >> 4
    sel_lag = (lax.broadcasted_iota(jnp.int32, (128, kl), 0) == lag_of_lane).astype(F32)
    ch_of_lane = lax.broadcasted_iota(jnp.int32, (SSM_H, kl), 1) & (SSM_H - 1)
    sel_ch = (lax.broadcasted_iota(jnp.int32, (SSM_H, kl), 0) == ch_of_lane).astype(F32)
    lag_of_row = lax.broadcasted_iota(jnp.int32, (kl, CHUNK), 0) >> 4
    sel_row = (lax.broadcasted_iota(jnp.int32, (kl, CHUNK), 1) == lag_of_row).astype(F32)
    for g in range(SSM_GROUPS_PER_STEP):
        y, h_fin = _ssm_group(
            u_ref[g], lre_c_ref[g], lim_c_ref[g], ldt_c_ref[g], lre_r_ref[g], lim_r_ref[g],
            ldt_r_ref[g], bta_ref[g], btb_ref[g], ctr_ref[g], cti_ref[g], dd_ref[g], h0_ref[g],
            sel_lag, sel_ch, sel_row, t_ref.at[g % 2], xs_ref)
        y_ref[g] = y
        hfin_ref[g] = h_fin


def _ssm(u, lre_c, lim_c, ldt_c, lre_r, lim_r, ldt_r, bta, btb, ctr, cti, ddiag, h0):
    gb = SSM_GROUPS_PER_STEP
    per_group = lambda *tail: pl.BlockSpec((gb,) + tail, lambda i: (i,) + (0,) * len(tail))
    return pl.pallas_call(
        _ssm_kernel,
        grid=(SSM_G // gb,),
        in_specs=[
            per_group(SSM_ROWS, CHUNK * SSM_H),
            per_group(SSM_P, 1), per_group(SSM_P, 1), per_group(SSM_P, 1),
            per_group(1, 128), per_group(1, 128), per_group(1, 128),
            per_group(SSM_H, 128), per_group(SSM_H, 128),
            per_group(SSM_P, SSM_H), per_group(SSM_P, SSM_H),
            per_group(SSM_H, SSM_H),
            per_group(8, 128),
        ],
        out_specs=[per_group(SSM_ROWS, CHUNK * SSM_H), per_group(16, 128)],
        out_shape=[
            jax.ShapeDtypeStruct((SSM_G, SSM_ROWS, CHUNK * SSM_H), BF16),
            jax.ShapeDtypeStruct((SSM_G, 16, 128), F32),
        ],
        scratch_shapes=[
            pltpu.VMEM((2, CHUNK * SSM_H, CHUNK * SSM_H), BF16),
            pltpu.VMEM((128 + 256, 128), F32),
        ],
        compiler_params=_params(1),
        name="ssm",
    )(u, lre_c, lim_c, ldt_c, lre_r, lim_r, ldt_r, bta, btb, ctr, cti, ddiag, h0)


def _merge_kernel(yb_ref, outa_ref, sga_ref, sgb_ref, x_ref, gluw_ref, glub_ref, pa_ref,
                  pb_ref, wo_ref, x1_ref):
    yb = yb_ref[...]
    glu = _dot(yb, gluw_ref[...]) + glub_ref[...]
    out_b = (yb.astype(F32) * jax.nn.sigmoid(glu)).astype(BF16)
    merged = (sga_ref[...].astype(F32) * _dot(outa_ref[...], pa_ref[...])
              + sgb_ref[...].astype(F32) * _dot(out_b, pb_ref[...]))
    x1_ref[...] = x_ref[...] + _dot(merged.astype(BF16), wo_ref[...])


def _merge(yb, outa, ug, x, glu_w, glu_b, proj_a, proj_b, w_out, *, tm):
    rows = x.shape[0]
    return pl.pallas_call(
        _merge_kernel,
        grid=(rows // tm,),
        in_specs=[
            pl.BlockSpec((tm, W_B), lambda i: (i, 0)),
            pl.BlockSpec((tm, W_A), lambda i: (i, 0)),
            pl.BlockSpec((tm, D_MODEL), lambda i: (i, 0)),
            pl.BlockSpec((tm, D_MODEL), lambda i: (i, 1)),
            pl.BlockSpec((tm, D_MODEL), lambda i: (i, 0)),
            _resident((W_B, W_B)),
            _resident((1, W_B)),
            _resident((W_A, D_MODEL)),
            _resident((W_B, D_MODEL)),
            _resident((D_MODEL, D_MODEL)),
        ],
        out_specs=pl.BlockSpec((tm, D_MODEL), lambda i: (i, 0)),
        out_shape=jax.ShapeDtypeStruct((rows, D_MODEL), F32),
        compiler_params=_params(1),
        name="merge",
    )(yb, outa, ug, ug, x, glu_w, glu_b, proj_a, proj_b, w_out)


def _ffn_kernel(x1_ref, g_ref, wug_ref, wuv_ref, cwg_ref, cwv_ref, bg_ref, bv_ref, wd_ref,
                gfin_ref, initg_ref, initv_ref, y_ref, coutg_ref, coutv_ref,
                hn_ref, extg0_ref, extg1_ref, extv0_ref, extv1_ref, act0_ref, act1_ref,
                carryg_ref, carryv_ref, *, segs, chain_tiles, n_f, n_items):
    t = pl.program_id(0)
    item1 = jnp.clip(t - 1, 0, n_items - 1)
    item2 = jnp.clip(t - 2, 0, n_items - 1)
    f1, tile1 = item1 % n_f, item1 // n_f
    f2 = item2 % n_f
    draining = t >= 2

    @pl.when(t == 0)
    def _():
        extg1_ref[...] = jnp.zeros(extg1_ref.shape, F32)
        extv1_ref[...] = jnp.zeros(extv1_ref.shape, F32)
        act0_ref[...] = jnp.zeros(act0_ref.shape, BF16)
        y_ref[...] = jnp.zeros(y_ref.shape, F32)

    @pl.when(jnp.minimum(t, n_items - 1) % n_f == 0)
    def _():
        hn_ref[...] = _rmsnorm(x1_ref[...], g_ref[...]).astype(BF16)

    @pl.when(jnp.logical_and(f2 == 0, draining))
    def _():
        y_ref[...] = x1_ref[...]

    _start_stream(initg_ref, carryg_ref.at[f1], tile1, chain_tiles)
    _start_stream(initv_ref, carryv_ref.at[f1], tile1, chain_tiles)

    bufs = ((extg0_ref, extv0_ref, act0_ref), (extg1_ref, extv1_ref, act1_ref))
    for par in (0, 1):
        @pl.when(t % 2 == par)
        def _(par=par):
            up_g, up_v, act_in = bufs[par]
            conv_src_g, conv_src_v, act_out = bufs[1 - par]

            def up_piece(dst_ref, w_ref, cols):
                return lambda: _stash_rows(dst_ref, _dot(hn_ref[...], w_ref[:, cols]), segs, cols)

            def down_piece(cols):
                def run():
                    y_ref[:, cols] += _dot(act_in[...], wd_ref[:, cols])
                return run

            def gate_piece(k, s, rows, cols):
                def run():
                    base = s + CARRY_ROWS * k
                    g = _conv_piece(conv_src_g, cwg_ref, base, rows, cols) + bg_ref[:, cols]
                    v = _conv_piece(conv_src_v, cwv_ref, base, rows, cols) + bv_ref[:, cols]
                    act_out[s + rows[0]:s + rows[1], cols] = (jax.nn.silu(g) * v).astype(BF16)
                return run

            col_slices = lambda width, step: [slice(c, c + step) for c in range(0, width, step)]
            mxu = ([(up_piece(up_g, wug_ref, c), 4) for c in col_slices(FF_TILE, MXU_COLS)]
                   + [(up_piece(up_v, wuv_ref, c), 4) for c in col_slices(FF_TILE, MXU_COLS)]
                   + [(down_piece(c), 1) for c in col_slices(D_MODEL, MXU_COLS)])
            vpu = [gate_piece(k, s, (r, min(r + GATE_ROWS, n)), c)
                   for c in col_slices(FF_TILE, 128)
                   for k, (s, n) in enumerate(segs) for r in range(0, n, GATE_ROWS)]

            _conv_prepare(conv_src_g, initg_ref, coutg_ref.at[:, f1], carryg_ref.at[f1], segs, chain_tiles)
            _conv_prepare(conv_src_v, initv_ref, coutv_ref.at[:, f1], carryv_ref.at[f1], segs, chain_tiles)
            total = sum(weight for _, weight in mxu)
            done, issued = 0, 0
            for run, weight in mxu:
                run()
                done += weight
                while issued < len(vpu) and issued * total < done * len(vpu):
                    vpu[issued]()
                    issued += 1

    @pl.when(jnp.logical_and(f2 == n_f - 1, draining))
    def _():
        y_ref[...] = _rmsnorm(y_ref[...], gfin_ref[...])


def _ffn(x1, g, w_up, conv_w, conv_b, w_down, g_fin, init_g, init_v, *, tm, segs, chain_tiles):
    rows = x1.shape[0]
    n_tiles = rows // tm
    n_f = D_FF // FF_TILE
    n_items = n_tiles * n_f
    nseg = len(segs)
    n_out_seq = n_tiles // chain_tiles if chain_tiles else 1
    item = lambda t, lag: jnp.clip(t - lag, 0, n_items - 1)
    ff = lambda lag: (lambda t: item(t, lag) % n_f)
    tile = lambda lag: (lambda t: item(t, lag) // n_f)
    seq1 = lambda t: tile(1)(t) // chain_tiles if chain_tiles else 0
    carry_block = (nseg, CARRY_ROWS, FF_TILE)
    cout_block = (nseg, n_f, CARRY_ROWS, FF_TILE)
    cout_shape = jax.ShapeDtypeStruct((n_out_seq * nseg, n_f, CARRY_ROWS, FF_TILE), F32)
    ext_shape = (tm + CARRY_ROWS * nseg, FF_TILE)
    y, cout_g, cout_v = pl.pallas_call(
        functools.partial(_ffn_kernel, segs=segs, chain_tiles=chain_tiles, n_f=n_f, n_items=n_items),
        grid=(n_items + 2,),
        in_specs=[
            pl.BlockSpec((tm, D_MODEL), lambda t: (tile(0)(t), 0)),
            pl.BlockSpec((1, D_MODEL), lambda t: (0, 0)),
            pl.BlockSpec((None, D_MODEL, FF_TILE), lambda t: (ff(0)(t), 0, 0)),
            pl.BlockSpec((None, D_MODEL, FF_TILE), lambda t: (n_f + ff(0)(t), 0, 0)),
            pl.BlockSpec((3, FF_TILE), lambda t: (0, ff(1)(t))),
            pl.BlockSpec((3, FF_TILE), lambda t: (0, n_f + ff(1)(t))),
            pl.BlockSpec((1, FF_TILE), lambda t: (0, ff(1)(t))),
            pl.BlockSpec((1, FF_TILE), lambda t: (0, n_f + ff(1)(t))),
            pl.BlockSpec((FF_TILE, D_MODEL), lambda t: (ff(2)(t), 0)),
            pl.BlockSpec((1, D_MODEL), lambda t: (0, 0)),
            pl.BlockSpec(carry_block, lambda t: (0, 0, ff(1)(t))),
            pl.BlockSpec(carry_block, lambda t: (0, 0, ff(1)(t))),
        ],
        out_specs=[
            pl.BlockSpec((tm, D_MODEL), lambda t: (tile(2)(t), 0)),
            pl.BlockSpec(cout_block, lambda t: (seq1(t), 0, 0, 0)),
            pl.BlockSpec(cout_block, lambda t: (seq1(t), 0, 0, 0)),
        ],
        out_shape=[jax.ShapeDtypeStruct((rows, D_MODEL), F32), cout_shape, cout_shape],
        scratch_shapes=[
            pltpu.VMEM((tm, D_MODEL), BF16),
            pltpu.VMEM(ext_shape, F32), pltpu.VMEM(ext_shape, F32),
            pltpu.VMEM(ext_shape, F32), pltpu.VMEM(ext_shape, F32),
            pltpu.VMEM((tm, FF_TILE), BF16), pltpu.VMEM((tm, FF_TILE), BF16),
            pltpu.VMEM((n_f, CARRY_ROWS, FF_TILE), F32),
            pltpu.VMEM((n_f, CARRY_ROWS, FF_TILE), F32),
        ],
        compiler_params=_params(1),
        name="ffn",
    )(x1, g, w_up, w_up, conv_w, conv_w, conv_b, conv_b, w_down, g_fin, init_g, init_v)
    widen = lambda c: c.transpose(0, 2, 1, 3).reshape(c.shape[0], CARRY_ROWS, D_FF)
    return y, widen(cout_g), widen(cout_v)


def _history_block(buf):
    return jnp.pad(buf, ((0, 0), (CARRY_ROWS - buf.shape[1], 0), (0, 0)))


def kernel(x_prompt, x_sample, cache_conv_a, state_ssm_re, state_ssm_im, cache_ffn_conv,
           meta_tokens, norm_mix_g, w_in, conv_a_w, ssm_lambda_re, ssm_lambda_im, ssm_log_dt,
           ssm_b_re, ssm_b_im, ssm_c_re, ssm_c_im, ssm_d, glu_w, glu_b, proj_a, proj_b,
           w_out, norm_ffn_g, w_up, ffn_conv_w, ffn_conv_b, w_down, norm_final_g):
    batch, seq, _ = x_prompt.shape
    dec_batch, dec_seq, _ = x_sample.shape
    assert dec_seq == CHUNK and seq % CHUNK == 0 and seq % ROW_TILE == 0 and seq % FFN_ROW_TILE == 0
    n_chunks = seq // CHUNK
    assert batch * n_chunks == 256 and dec_batch == 8

    w_a = w_in[0, :, :3 * W_A].astype(BF16)
    w_ug = jnp.concatenate([w_in[0, :, 3 * W_A + W_B:], w_in[0, :, 3 * W_A:3 * W_A + W_B]],
                           axis=1).astype(BF16)
    glu_w16, proj_a16, proj_b16 = glu_w[0].astype(BF16), proj_a[0].astype(BF16), proj_b[0].astype(BF16)
    w_out16, w_down16 = w_out[0].astype(BF16), w_down[0].astype(BF16)
    w_up16 = w_up[0].astype(BF16).reshape(D_MODEL, 2 * D_FF // FF_TILE, FF_TILE).transpose(1, 0, 2)
    g_mix, g_ffn, g_fin = norm_mix_g[0][None], norm_ffn_g[0][None], norm_final_g[None]
    glu_b2, ffn_b2 = glu_b[0][None], ffn_conv_b[0][None]

    n_s = dec_batch * dec_seq
    rows_s = n_s + N_META
    x_s = jnp.concatenate([x_sample.reshape(n_s, D_MODEL), meta_tokens], axis=0)
    segs_s = tuple((b * dec_seq, dec_seq) for b in range(dec_batch)) + ((n_s, N_META),)
    x_p = x_prompt.reshape(batch * seq, D_MODEL)
    segs_p = ((0, ROW_TILE),)
    chain = seq // ROW_TILE

    zero_hist = lambda c: jnp.zeros((1, CARRY_ROWS, c), F32)
    init_a_s = jnp.concatenate([_history_block(cache_conv_a[0]), zero_hist(W_A)], axis=0)
    hn_s, outa_s, ca_s = _mixer_a(x_s, g_mix, w_a, conv_a_w[0], init_a_s,
                                  tm=rows_s, segs=segs_s, chain_tiles=0)
    hn_p, outa_p, ca_p = _mixer_a(x_p, g_mix, w_a, conv_a_w[0], ca_s[dec_batch:],
                                  tm=ROW_TILE, segs=segs_p, chain_tiles=chain)
    ug_s = _gates_u(hn_s, w_ug, tm=rows_s)
    ug_p = _gates_u(hn_p, w_ug, tm=ROW_TILE)

    u_p = ug_p[:, 2 * D_MODEL:].reshape(batch, n_chunks, CHUNK, SSM_G, SSM_H)
    u_p = u_p.transpose(3, 1, 0, 2, 4).reshape(SSM_G, n_chunks * batch, CHUNK * SSM_H)
    u_s = ug_s[:n_s, 2 * D_MODEL:].reshape(dec_batch, CHUNK, SSM_G, SSM_H)
    u_s = u_s.transpose(2, 0, 1, 3).reshape(SSM_G, dec_batch, CHUNK * SSM_H)
    u_m = ug_s[n_s:, 2 * D_MODEL:].reshape(N_META, SSM_G, SSM_H).transpose(1, 0, 2)
    u_m = jnp.pad(u_m, ((0, 0), (CHUNK - N_META, 0), (0, 0))).reshape(SSM_G, 1, CHUNK * SSM_H)
    pad_rows = SSM_ROWS - n_chunks * batch - dec_batch - 1
    u_all = jnp.concatenate(
        [u_p, u_s, u_m, jnp.zeros((SSM_G, pad_rows, CHUNK * SSM_H), BF16)], axis=1)

    dup = lambda v: jnp.concatenate([v, v], axis=-1)[:, None, :]
    lre, lim = ssm_lambda_re[0], ssm_lambda_im[0]
    ldt = jnp.broadcast_to(ssm_log_dt[0][:, None], (SSM_G, SSM_P))
    bt_re, bt_im = ssm_b_re[0].swapaxes(1, 2), ssm_b_im[0].swapaxes(1, 2)
    bta = jnp.concatenate([bt_re, bt_im], axis=-1)
    btb = jnp.concatenate([-bt_im, bt_re], axis=-1)
    ddiag = ssm_d[0][:, :, None] * jnp.eye(SSM_H, dtype=F32)
    h0 = jnp.concatenate([state_ssm_re[0], state_ssm_im[0]], axis=-1).swapaxes(0, 1)
    y_all, h_fin = _ssm(u_all, lre[:, :, None], lim[:, :, None], ldt[:, :, None],
                        dup(lre), dup(lim), dup(ldt), bta, btb,
                        ssm_c_re[0].swapaxes(1, 2), ssm_c_im[0].swapaxes(1, 2), ddiag, h0)

    n_pc = n_chunks * batch
    yb_p = y_all[:, :n_pc].reshape(SSM_G, n_chunks, batch, CHUNK, SSM_H)
    yb_p = yb_p.transpose(2, 1, 3, 0, 4).reshape(batch * seq, W_B)
    yb_s = y_all[:, n_pc:n_pc + dec_batch].reshape(SSM_G, dec_batch, CHUNK, SSM_H)
    yb_s = yb_s.transpose(1, 2, 0, 3).reshape(n_s, W_B)
    yb_m = y_all[:, n_pc + dec_batch].reshape(SSM_G, CHUNK, SSM_H)[:, CHUNK - N_META:]
    yb_s = jnp.concatenate([yb_s, yb_m.transpose(1, 0, 2).reshape(N_META, W_B)], axis=0)

    x1_s = _merge(yb_s, outa_s, ug_s, x_s, glu_w16, glu_b2, proj_a16, proj_b16, w_out16, tm=rows_s)
    x1_p = _merge(yb_p, outa_p, ug_p, x_p, glu_w16, glu_b2, proj_a16, proj_b16, w_out16, tm=ROW_TILE)

    hist_f = _history_block(cache_ffn_conv[0])
    init_g_s = jnp.concatenate([hist_f[:, :, :D_FF], zero_hist(D_FF)], axis=0)
    init_v_s = jnp.concatenate([hist_f[:, :, D_FF:], zero_hist(D_FF)], axis=0)
    y_s, cg_s, cv_s = _ffn(x1_s, g_ffn, w_up16, ffn_conv_w[0], ffn_b2, w_down16, g_fin,
                           init_g_s, init_v_s, tm=rows_s, segs=segs_s, chain_tiles=0)
    y_p, cg_p, cv_p = _ffn(x1_p, g_ffn, w_up16, ffn_conv_w[0], ffn_b2, w_down16, g_fin,
                           cg_s[dec_batch:], cv_s[dec_batch:], tm=FFN_ROW_TILE,
                           segs=((0, FFN_ROW_TILE),), chain_tiles=seq // FFN_ROW_TILE)

    hist = lambda c: c[:, CARRY_ROWS - 2:][None]
    ffn_hist = lambda cg, cv: hist(jnp.concatenate([cg, cv], axis=-1))
    state = lambda h: (h[:, :, :SSM_P].swapaxes(0, 1)[None], h[:, :, SSM_P:].swapaxes(0, 1)[None])
    p_re, p_im = state(h_fin[:, 0:batch])
    s_re, s_im = state(h_fin[:, 8:8 + dec_batch])
    return (y_p.reshape(batch, seq, D_MODEL), y_s[:n_s].reshape(dec_batch, dec_seq, D_MODEL),
            hist(ca_p), p_re, p_im, ffn_hist(cg_p, cv_p),
            hist(ca_s[:dec_batch]), s_re, s_im, ffn_hist(cg_s[:dec_batch], cv_s[:dec_batch]))
```

```python
import functools

import jax
import jax.numpy as jnp
from jax import lax
from jax.experimental import pallas as pl
from jax.experimental.pallas import tpu as pltpu

D_MODEL = 2048
W_A = 1024
W_B = 1024
SSM_H = 16
SSM_G = 64
SSM_P = 64
D_FF = 5632
CHUNK = 64
N_META = 16
EPS = 1e-6

BF16 = jnp.bfloat16
F32 = jnp.float32
HIGHEST = lax.Precision.HIGHEST

CARRY_ROWS = 8
ROW_TILE = 512
FFN_ROW_TILE = 512
FF_TILE = 512
MXU_COLS = 256
GATE_ROWS = 128
SLAB = 128
SLAB_GROUPS = SLAB // SSM_H
SSM_ROWS = 272
VMEM_LIMIT = 56 * 1024 * 1024


def _dot(a, b):
    return jnp.dot(a, b, preferred_element_type=F32)


def _dot_hi(a, b):
    return jnp.dot(a, b, preferred_element_type=F32, precision=HIGHEST)


def _rmsnorm(x, g):
    ms = jnp.mean(x * x, axis=-1, keepdims=True)
    return (x * lax.rsqrt(ms + EPS)) * g


def _params(n_axes):
    return pltpu.CompilerParams(
        dimension_semantics=("arbitrary",) * n_axes, vmem_limit_bytes=VMEM_LIMIT)


def _resident(shape):
    return pl.BlockSpec(shape, lambda *_: (0,) * len(shape), pipeline_mode=pl.Buffered(1))


def _stash_rows(ext_ref, v, segs, cols=slice(None)):
    for k, (s, n) in enumerate(segs):
        base = s + CARRY_ROWS * (k + 1)
        ext_ref[base:base + n, cols] = v[s:s + n]


def _conv_prepare(ext_ref, init_ref, cout_ref, carry_ref, segs, chain_tiles):
    for k, (s, n) in enumerate(segs):
        base = s + CARRY_ROWS * k
        ext_ref[base:base + CARRY_ROWS, :] = carry_ref[...] if chain_tiles else init_ref[k]
        last = ext_ref[base + n:base + n + CARRY_ROWS, :]
        cout_ref[k] = last
        if chain_tiles:
            carry_ref[...] = last


def _conv_piece(ext_ref, w_ref, base, rows, cols=slice(None)):
    r0, r1 = rows
    t2 = ext_ref[base + 6 + r0:base + 6 + r1, cols]
    t1 = ext_ref[base + 7 + r0:base + 7 + r1, cols]
    t0 = ext_ref[base + 8 + r0:base + 8 + r1, cols]
    return t2 * w_ref[0:1, cols] + t1 * w_ref[1:2, cols] + t0 * w_ref[2:3, cols]


def _start_stream(init_ref, carry_ref, tile, chain_tiles):
    if chain_tiles:
        @pl.when(tile % chain_tiles == 0)
        def _():
            carry_ref[...] = init_ref[0]


def _mixer_a_kernel(x_ref, g_ref, w_ref, cw_ref, init_ref, hn_ref, outa_ref, cout_ref,
                    ext_ref, carry_ref, *, segs, chain_tiles):
    _start_stream(init_ref, carry_ref, pl.program_id(0), chain_tiles)
    hn = _rmsnorm(x_ref[...], g_ref[...]).astype(BF16)
    hn_ref[...] = hn
    zb = _dot(hn, w_ref[:, 0:W_A])
    zc = _dot(hn, w_ref[:, W_A:2 * W_A])
    zh = _dot(hn, w_ref[:, 2 * W_A:3 * W_A])
    _stash_rows(ext_ref, zc * zh, segs)
    _conv_prepare(ext_ref, init_ref, cout_ref, carry_ref, segs, chain_tiles)
    for k, (s, n) in enumerate(segs):
        conv = _conv_piece(ext_ref, cw_ref, s + CARRY_ROWS * k, (0, n))
        outa_ref[s:s + n, :] = (zb[s:s + n] * conv).astype(BF16)


def _mixer_a(x, g, w_a, conv_w, init, *, tm, segs, chain_tiles):
    rows = x.shape[0]
    n_tiles = rows // tm
    nseg = len(segs)
    n_out_seq = n_tiles // chain_tiles if chain_tiles else 1
    seq = (lambda i: (i // chain_tiles, 0, 0)) if chain_tiles else (lambda i: (0, 0, 0))
    return pl.pallas_call(
        functools.partial(_mixer_a_kernel, segs=segs, chain_tiles=chain_tiles),
        grid=(n_tiles,),
        in_specs=[
            pl.BlockSpec((tm, D_MODEL), lambda i: (i, 0)),
            _resident((1, D_MODEL)),
            _resident((D_MODEL, 3 * W_A)),
            _resident((3, W_A)),
            pl.BlockSpec((nseg, CARRY_ROWS, W_A), lambda i: (0, 0, 0)),
        ],
        out_specs=[
            pl.BlockSpec((tm, D_MODEL), lambda i: (i, 0)),
            pl.BlockSpec((tm, W_A), lambda i: (i, 0)),
            pl.BlockSpec((nseg, CARRY_ROWS, W_A), seq),
        ],
        out_shape=[
            jax.ShapeDtypeStruct((rows, D_MODEL), BF16),
            jax.ShapeDtypeStruct((rows, W_A), BF16),
            jax.ShapeDtypeStruct((n_out_seq * nseg, CARRY_ROWS, W_A), F32),
        ],
        scratch_shapes=[
            pltpu.VMEM((tm + CARRY_ROWS * nseg, W_A), F32),
            pltpu.VMEM((CARRY_ROWS, W_A), F32),
        ],
        compiler_params=_params(1),
        name="mixer_a",
    )(x, g, w_a, conv_w, init)


def _transpose_blocks(x):
    rows = x.shape[0]
    shape3 = (rows // 8, 8, SLAB)
    blk = lax.broadcasted_iota(jnp.int32, shape3, 2) >> 4
    x = pltpu.roll(x, 0, 1, stride=SSM_H, stride_axis=0)
    x3 = x.reshape(shape3)
    for d in (4, 2, 1):
        x3 = jnp.where((blk & d) != 0, pltpu.roll(x3, 8 - d, 1), x3)
    x3 = pltpu.roll(x3.reshape(rows, SLAB), 0, 1, stride=SSM_H, stride_axis=0).reshape(shape3)
    r = lax.broadcasted_iota(jnp.int32, shape3, 1) & 3
    x3 = jnp.where(r == 0, x3, jnp.where(r == 1, pltpu.roll(x3, 2, 1),
                                        jnp.where(r == 2, pltpu.roll(x3, 4, 1), pltpu.roll(x3, 6, 1))))
    return x3.reshape(rows, SLAB)


GATE_COLS = 2 * D_MODEL


def _gates_u_kernel(hn_ref, w_ref, gates_ref, u_ref):
    hn = hn_ref[...]
    u = _dot(hn, w_ref[:, GATE_COLS:])
    n_gate = GATE_COLS // 1024
    slabs_per_gate = W_B // SLAB // n_gate
    for k in range(n_gate):
        cols = slice(k * 1024, (k + 1) * 1024)
        gates_ref[:, cols] = jax.nn.sigmoid(_dot(hn, w_ref[:, cols])).astype(BF16)
        for j in range(k * slabs_per_gate, (k + 1) * slabs_per_gate):
            lanes = slice(j * SLAB, (j + 1) * SLAB)
            u_ref[:, lanes] = _transpose_blocks(u[:, lanes])


def _gates_u(hn, w_ug, *, tm):
    rows = hn.shape[0]
    return pl.pallas_call(
        _gates_u_kernel,
        grid=(rows // tm,),
        in_specs=[
            pl.BlockSpec((tm, D_MODEL), lambda i: (i, 0)),
            _resident((D_MODEL, GATE_COLS + W_B)),
        ],
        out_specs=[pl.BlockSpec((tm, GATE_COLS), lambda i: (i, 0)),
                   pl.BlockSpec((tm, W_B), lambda i: (i, 0))],
        out_shape=[jax.ShapeDtypeStruct((rows, GATE_COLS), BF16),
                   jax.ShapeDtypeStruct((rows, W_B), F32)],
        compiler_params=_params(1),
        name="gates_u",
    )(hn, w_ug)


def _ssm_group(u, lre_c, lim_c, ldt_c, lre_r, lim_r, ldt_r, bta, btb, ctr, cti, ddiag, h0,
               sel_lag, sel_ch, sel_row, t_ref, xs_ref):
    lane128 = lax.broadcasted_iota(jnp.int32, (1, 128), 1)
    sgn_conj = jnp.where(lane128 < SSM_P, 1.0, -1.0).astype(F32)
    sgn_mul = -sgn_conj

    def swap(h):
        return pltpu.roll(h, SSM_P, axis=1)

    def cmul(ar, ai, h):
        return ar * h + (ai * sgn_mul) * swap(h)

    dt_c = jnp.exp(ldt_c)
    k_row = lax.broadcasted_iota(jnp.int32, (1, 128), 1).astype(F32)
    mag = jnp.exp((lre_c * dt_c) * k_row)
    ang = (lim_c * dt_c) * k_row
    apt_re = mag * jnp.cos(ang)
    apt_im = mag * jnp.sin(ang)
    ae_re = _dot_hi(apt_re, sel_lag)
    ae_im = _dot_hi(apt_im, sel_lag)
    cte_re = _dot_hi(ctr, sel_ch)
    cte_im = _dot_hi(cti, sel_ch)
    ca_re = cte_re * ae_re - cte_im * ae_im
    ca_im = cte_re * ae_im + cte_im * ae_re
    a1_re = apt_re[:, 1:2]
    a1_im = apt_im[:, 1:2]
    ca1_re = ca_re * a1_re - ca_im * a1_im
    ca1_im = ca_re * a1_im + ca_im * a1_re

    dt_r = jnp.exp(ldt_r)
    xr = lre_r * dt_r
    xi = lim_r * dt_r
    a_re = jnp.exp(xr) * jnp.cos(xi)
    a_im = jnp.exp(xr) * jnp.sin(xi)
    den = lre_r * lre_r + lim_r * lim_r
    nr = a_re - 1.0
    cr = (nr * lre_r + a_im * lim_r) / den
    ci = (a_im * lre_r - nr * lim_r) / den
    bb_a = bta * cr + btb * ci
    bb_b = btb * cr - bta * ci

    k2 = _dot_hi(bb_a * sgn_conj, jnp.concatenate([ca_re, ca_im], axis=0))
    lane1k = lax.broadcasted_iota(jnp.int32, (SSM_H, CHUNK * SSM_H), 1)
    k2 = k2 + jnp.where(lane1k < SSM_H, _dot_hi(ddiag, sel_ch), 0.0)
    k2z = jnp.concatenate([k2, jnp.zeros_like(k2)], axis=1)
    for r in range(8):
        kr = pltpu.roll(k2z, SSM_H * r, axis=1) if r else k2z
        for q in range(CHUNK // 8):
            m = 8 * q + r
            rows = slice(SSM_H * m, SSM_H * (m + 1))
            if q:
                t_ref[rows, 0:128 * q] = jnp.zeros((SSM_H, 128 * q), BF16)
            t_ref[rows, 128 * q:] = kr[:, 0:CHUNK * SSM_H - 128 * q].astype(BF16)

    e_col = (CHUNK - 1 - lax.broadcasted_iota(jnp.int32, (CHUNK, 1), 0)).astype(F32)
    apm = jnp.exp(e_col * xr)
    apa = e_col * xi
    apx_re = _dot_hi(sel_row, apm * jnp.cos(apa))
    apx_im = _dot_hi(sel_row, apm * jnp.sin(apa))
    tile_rows = lambda b: jnp.broadcast_to(b[None], (CHUNK, SSM_H, 128)).reshape(CHUNK * SSM_H, 128)
    bcat = (apx_re * tile_rows(bb_a) + apx_im * tile_rows(bb_b)).astype(BF16)
    ccat = jnp.concatenate([ca1_re, -ca1_im], axis=0).astype(BF16)

    s = _dot(u, bcat)
    al_re = jnp.exp(CHUNK * xr) * jnp.cos(CHUNK * xi)
    al_im = jnp.exp(CHUNK * xr) * jnp.sin(CHUNK * xi)

    h_meta = s[264:265]
    h_sample = cmul(al_re, al_im, h0) + s[256:264]

    n_p, n_c = 256, 128
    chunk = lax.broadcasted_iota(jnp.int32, (n_p, 128), 0) & (n_c - 1)
    x = s[0:n_p] + jnp.where(chunk == 0, cmul(al_re, al_im, h_meta), 0.0)
    head = n_c
    xs_ref[0:head, :] = jnp.zeros((head, 128), F32)
    ar, ai = al_re, al_im
    for step in range(7):
        sh = 1 << step
        xs_ref[head:head + n_p, :] = x
        x = x + jnp.where(chunk >= sh, cmul(ar, ai, xs_ref[head - sh:head + n_p - sh, :]), 0.0)
        ar, ai = ar * ar - ai * ai, 2.0 * ar * ai
    xs_ref[head:head + n_p, :] = x
    h_in_prompt = jnp.where(chunk == 0, h_meta, xs_ref[head - 1:head + n_p - 1, :])
    h_in = jnp.concatenate([h_in_prompt, h0, jnp.zeros((SSM_ROWS - n_p - 8, 128), F32)], axis=0)

    y = _dot(u, t_ref[...]) + _dot(h_in.astype(BF16), ccat)
    h_fin = jnp.concatenate([x[n_c - 1:n_c], x[n_p - 1:n_p], jnp.zeros((6, 128), F32), h_sample], axis=0)
    return y, h_fin


def _ssm_kernel(up_ref, us_ref, lre_c_ref, lim_c_ref, ldt_c_ref, lre_r_ref, lim_r_ref, ldt_r_ref,
                bta_ref, btb_ref, ctr_ref, cti_ref, dd_ref, h0_ref, yp_ref, ys_ref, hfin_ref,
                t_ref, xs_ref, wp_ref, ws_ref, *, n_prompt_chunks, n_sample_chunks):
    kl = CHUNK * SSM_H
    lag_of_lane = lax.broadcasted_iota(jnp.int32, (128, kl), 1) >> 4
    sel_lag = (lax.broadcasted_iota(jnp.int32, (128, kl), 0) == lag_of_lane).astype(F32)
    ch_of_lane = lax.broadcasted_iota(jnp.int32, (SSM_H, kl), 1) & (SSM_H - 1)
    sel_ch = (lax.broadcasted_iota(jnp.int32, (SSM_H, kl), 0) == ch_of_lane).astype(F32)
    lag_of_row = lax.broadcasted_iota(jnp.int32, (kl, CHUNK), 0) >> 4
    sel_row = (lax.broadcasted_iota(jnp.int32, (kl, CHUNK), 1) == lag_of_row).astype(F32)
    octets = CHUNK // 8
    meta_row = 8 * octets * n_sample_chunks
    meta_octets = N_META // 8
    first_row = lax.broadcasted_iota(jnp.int32, (8, kl), 0) == 0

    def chunk_rows(slot, i, n):
        return pl.ds(8 * i + slot, n, stride=CHUNK)

    for s in range(SLAB_GROUPS):
        slot = s
        u_p = jnp.concatenate([up_ref[chunk_rows(slot, i, n_prompt_chunks), :]
                               for i in range(octets)], axis=1)
        u_s = jnp.concatenate([us_ref[chunk_rows(slot, i, n_sample_chunks), :]
                               for i in range(octets)], axis=1)
        u_m = jnp.concatenate(
            [jnp.zeros((1, kl - SLAB * meta_octets), F32)]
            + [us_ref[pl.ds(meta_row + 8 * i + slot, 1), :] for i in range(meta_octets)], axis=1)
        u = jnp.concatenate([u_p, u_s, jnp.where(first_row, u_m, 0.0)], axis=0).astype(BF16)
        y, h_fin = _ssm_group(
            u, lre_c_ref[s], lim_c_ref[s], ldt_c_ref[s], lre_r_ref[s], lim_r_ref[s],
            ldt_r_ref[s], bta_ref[s], btb_ref[s], ctr_ref[s], cti_ref[s], dd_ref[s], h0_ref[s],
            sel_lag, sel_ch, sel_row, t_ref.at[s % 2], xs_ref)
        hfin_ref[s] = h_fin
        n_ps = n_prompt_chunks + n_sample_chunks
        for i in range(octets):
            lanes = slice(SLAB * i, SLAB * (i + 1))
            wp_ref[chunk_rows(slot, i, n_prompt_chunks), :] = y[0:n_prompt_chunks, lanes]
            ws_ref[chunk_rows(slot, i, n_sample_chunks), :] = y[n_prompt_chunks:n_ps, lanes]
        for i in range(meta_octets):
            lanes = slice(kl - SLAB * (meta_octets - i), kl - SLAB * (meta_octets - i - 1))
            ws_ref[pl.ds(meta_row + 8 * i + slot, 1), :] = y[n_ps:n_ps + 1, lanes]

    finish = lambda w: jax.nn.gelu(_transpose_blocks(w)).astype(BF16)
    ys_ref[...] = finish(ws_ref[...])
    step = 1024
    for r in range(0, wp_ref.shape[0], step):
        yp_ref[r:r + step, :] = finish(wp_ref[r:r + step, :])


def _ssm(u_p, u_s, lre_c, lim_c, ldt_c, lre_r, lim_r, ldt_r, bta, btb, ctr, cti, ddiag, h0,
         *, n_prompt_chunks, n_sample_chunks):
    gb = SLAB_GROUPS
    per_group = lambda *tail: pl.BlockSpec((gb,) + tail, lambda j: (j,) + (0,) * len(tail))
    slab = lambda rows: pl.BlockSpec((rows, SLAB), lambda j: (0, j))
    rows_p, rows_s = u_p.shape[0], u_s.shape[0]
    return pl.pallas_call(
        functools.partial(_ssm_kernel, n_prompt_chunks=n_prompt_chunks, n_sample_chunks=n_sample_chunks),
        grid=(SSM_G // gb,),
        in_specs=[
            slab(rows_p), slab(rows_s),
            per_group(SSM_P, 1), per_group(SSM_P, 1), per_group(SSM_P, 1),
            per_group(1, 128), per_group(1, 128), per_group(1, 128),
            per_group(SSM_H, 128), per_group(SSM_H, 128),
            per_group(SSM_P, SSM_H), per_group(SSM_P, SSM_H),
            per_group(SSM_H, SSM_H),
            per_group(8, 128),
        ],
        out_specs=[slab(rows_p), slab(rows_s), per_group(16, 128)],
        out_shape=[
            jax.ShapeDtypeStruct((rows_p, W_B), BF16),
            jax.ShapeDtypeStruct((rows_s, W_B), BF16),
            jax.ShapeDtypeStruct((SSM_G, 16, 128), F32),
        ],
        scratch_shapes=[
            pltpu.VMEM((2, CHUNK * SSM_H, CHUNK * SSM_H), BF16),
            pltpu.VMEM((128 + 256, 128), F32),
            pltpu.VMEM((rows_p, SLAB), F32),
            pltpu.VMEM((rows_s, SLAB), F32),
        ],
        compiler_params=_params(1),
        name="ssm",
    )(u_p, u_s, lre_c, lim_c, ldt_c, lre_r, lim_r, ldt_r, bta, btb, ctr, cti, ddiag, h0)


def _merge_kernel(yb_ref, outa_ref, sga_ref, sgb_ref, x_ref, gluw_ref, glub_ref, pa_ref,
                  pb_ref, wo_ref, x1_ref):
    yb = yb_ref[...]
    glu = _dot(yb, gluw_ref[...]) + glub_ref[...]
    out_b = (yb.astype(F32) * jax.nn.sigmoid(glu)).astype(BF16)
    merged = (sga_ref[...].astype(F32) * _dot(outa_ref[...], pa_ref[...])
              + sgb_ref[...].astype(F32) * _dot(out_b, pb_ref[...]))
    x1_ref[...] = x_ref[...] + _dot(merged.astype(BF16), wo_ref[...])


def _merge(yb, outa, ug, x, glu_w, glu_b, proj_a, proj_b, w_out, *, tm):
    rows = x.shape[0]
    return pl.pallas_call(
        _merge_kernel,
        grid=(rows // tm,),
        in_specs=[
            pl.BlockSpec((tm, W_B), lambda i: (i, 0)),
            pl.BlockSpec((tm, W_A), lambda i: (i, 0)),
            pl.BlockSpec((tm, D_MODEL), lambda i: (i, 0)),
            pl.BlockSpec((tm, D_MODEL), lambda i: (i, 1)),
            pl.BlockSpec((tm, D_MODEL), lambda i: (i, 0)),
            _resident((W_B, W_B)),
            _resident((1, W_B)),
            _resident((W_A, D_MODEL)),
            _resident((W_B, D_MODEL)),
            _resident((D_MODEL, D_MODEL)),
        ],
        out_specs=pl.BlockSpec((tm, D_MODEL), lambda i: (i, 0)),
        out_shape=jax.ShapeDtypeStruct((rows, D_MODEL), F32),
        compiler_params=_params(1),
        name="merge",
    )(yb, outa, ug, ug, x, glu_w, glu_b, proj_a, proj_b, w_out)


def _ffn_kernel(x1_ref, g_ref, wug_ref, wuv_ref, cwg_ref, cwv_ref, bg_ref, bv_ref, wd_ref,
                gfin_ref, initg_ref, initv_ref, y_ref, coutg_ref, coutv_ref,
                hn_ref, extg0_ref, extg1_ref, extv0_ref, extv1_ref, act0_ref, act1_ref,
                carryg_ref, carryv_ref, *, segs, chain_tiles, n_f, n_items):
    t = pl.program_id(0)
    item1 = jnp.clip(t - 1, 0, n_items - 1)
    item2 = jnp.clip(t - 2, 0, n_items - 1)
    f1, tile1 = item1 % n_f, item1 // n_f
    f2 = item2 % n_f
    draining = t >= 2

    @pl.when(t == 0)
    def _():
        extg1_ref[...] = jnp.zeros(extg1_ref.shape, F32)
        extv1_ref[...] = jnp.zeros(extv1_ref.shape, F32)
        act0_ref[...] = jnp.zeros(act0_ref.shape, BF16)
        y_ref[...] = jnp.zeros(y_ref.shape, F32)

    @pl.when(jnp.minimum(t, n_items - 1) % n_f == 0)
    def _():
        hn_ref[...] = _rmsnorm(x1_ref[...], g_ref[...]).astype(BF16)

    @pl.when(jnp.logical_and(f2 == 0, draining))
    def _():
        y_ref[...] = x1_ref[...]

    _start_stream(initg_ref, carryg_ref.at[f1], tile1, chain_tiles)
    _start_stream(initv_ref, carryv_ref.at[f1], tile1, chain_tiles)

    bufs = ((extg0_ref, extv0_ref, act0_ref), (extg1_ref, extv1_ref, act1_ref))
    for par in (0, 1):
        @pl.when(t % 2 == par)
        def _(par=par):
            up_g, up_v, act_in = bufs[par]
            conv_src_g, conv_src_v, act_out = bufs[1 - par]

            def up_piece(dst_ref, w_ref, cols):
                return lambda: _stash_rows(dst_ref, _dot(hn_ref[...], w_ref[:, cols]), segs, cols)

            def down_piece(cols):
                def run():
                    y_ref[:, cols] += _dot(act_in[...], wd_ref[:, cols])
                return run

            def gate_piece(k, s, rows, cols):
                def run():
                    base = s + CARRY_ROWS * k
                    g = _conv_piece(conv_src_g, cwg_ref, base, rows, cols) + bg_ref[:, cols]
                    v = _conv_piece(conv_src_v, cwv_ref, base, rows, cols) + bv_ref[:, cols]
                    act_out[s + rows[0]:s + rows[1], cols] = (jax.nn.silu(g) * v).astype(BF16)
                return run

            col_slices = lambda width, step: [slice(c, c + step) for c in range(0, width, step)]
            mxu = ([(up_piece(up_g, wug_ref, c), 4) for c in col_slices(FF_TILE, MXU_COLS)]
                   + [(up_piece(up_v, wuv_ref, c), 4) for c in col_slices(FF_TILE, MXU_COLS)]
                   + [(down_piece(c), 1) for c in col_slices(D_MODEL, MXU_COLS)])
            vpu = [gate_piece(k, s, (r, min(r + GATE_ROWS, n)), c)
                   for c in col_slices(FF_TILE, 128)
                   for k, (s, n) in enumerate(segs) for r in range(0, n, GATE_ROWS)]

            _conv_prepare(conv_src_g, initg_ref, coutg_ref.at[:, f1], carryg_ref.at[f1], segs, chain_tiles)
            _conv_prepare(conv_src_v, initv_ref, coutv_ref.at[:, f1], carryv_ref.at[f1], segs, chain_tiles)
            total = sum(weight for _, weight in mxu)
            done, issued = 0, 0
            for run, weight in mxu:
                run()
                done += weight
                while issued < len(vpu) and issued * total < done * len(vpu):
                    vpu[issued]()
                    issued += 1

    @pl.when(jnp.logical_and(f2 == n_f - 1, draining))
    def _():
        y_ref[...] = _rmsnorm(y_ref[...], gfin_ref[...])


def _ffn(x1, g, w_up, conv_w, conv_b, w_down, g_fin, init_g, init_v, *, tm, segs, chain_tiles):
    rows = x1.shape[0]
    n_tiles = rows // tm
    n_f = D_FF // FF_TILE
    n_items = n_tiles * n_f
    nseg = len(segs)
    n_out_seq = n_tiles // chain_tiles if chain_tiles else 1
    item = lambda t, lag: jnp.clip(t - lag, 0, n_items - 1)
    ff = lambda lag: (lambda t: item(t, lag) % n_f)
    tile = lambda lag: (lambda t: item(t, lag) // n_f)
    seq1 = lambda t: tile(1)(t) // chain_tiles if chain_tiles else 0
    carry_block = (nseg, CARRY_ROWS, FF_TILE)
    cout_block = (nseg, n_f, CARRY_ROWS, FF_TILE)
    cout_shape = jax.ShapeDtypeStruct((n_out_seq * nseg, n_f, CARRY_ROWS, FF_TILE), F32)
    ext_shape = (tm + CARRY_ROWS * nseg, FF_TILE)
    y, cout_g, cout_v = pl.pallas_call(
        functools.partial(_ffn_kernel, segs=segs, chain_tiles=chain_tiles, n_f=n_f, n_items=n_items),
        grid=(n_items + 2,),
        in_specs=[
            pl.BlockSpec((tm, D_MODEL), lambda t: (tile(0)(t), 0)),
            pl.BlockSpec((1, D_MODEL), lambda t: (0, 0)),
            pl.BlockSpec((D_MODEL, FF_TILE), lambda t: (0, ff(0)(t))),
            pl.BlockSpec((D_MODEL, FF_TILE), lambda t: (0, n_f + ff(0)(t))),
            pl.BlockSpec((3, FF_TILE), lambda t: (0, ff(1)(t))),
            pl.BlockSpec((3, FF_TILE), lambda t: (0, n_f + ff(1)(t))),
            pl.BlockSpec((1, FF_TILE), lambda t: (0, ff(1)(t))),
            pl.BlockSpec((1, FF_TILE), lambda t: (0, n_f + ff(1)(t))),
            pl.BlockSpec((FF_TILE, D_MODEL), lambda t: (ff(2)(t), 0)),
            pl.BlockSpec((1, D_MODEL), lambda t: (0, 0)),
            pl.BlockSpec(carry_block, lambda t: (0, 0, ff(1)(t))),
            pl.BlockSpec(carry_block, lambda t: (0, 0, ff(1)(t))),
        ],
        out_specs=[
            pl.BlockSpec((tm, D_MODEL), lambda t: (tile(2)(t), 0)),
            pl.BlockSpec(cout_block, lambda t: (seq1(t), 0, 0, 0)),
            pl.BlockSpec(cout_block, lambda t: (seq1(t), 0, 0, 0)),
        ],
        out_shape=[jax.ShapeDtypeStruct((rows, D_MODEL), F32), cout_shape, cout_shape],
        scratch_shapes=[
            pltpu.VMEM((tm, D_MODEL), BF16),
            pltpu.VMEM(ext_shape, F32), pltpu.VMEM(ext_shape, F32),
            pltpu.VMEM(ext_shape, F32), pltpu.VMEM(ext_shape, F32),
            pltpu.VMEM((tm, FF_TILE), BF16), pltpu.VMEM((tm, FF_TILE), BF16),
            pltpu.VMEM((n_f, CARRY_ROWS, FF_TILE), F32),
            pltpu.VMEM((n_f, CARRY_ROWS, FF_TILE), F32),
        ],
        compiler_params=_params(1),
        name="ffn",
    )(x1, g, w_up, w_up, conv_w, conv_w, conv_b, conv_b, w_down, g_fin, init_g, init_v)
    widen = lambda c: c.transpose(0, 2, 1, 3).reshape(c.shape[0], CARRY_ROWS, D_FF)
    return y, widen(cout_g), widen(cout_v)


def _history_block(buf):
    return jnp.pad(buf, ((0, 0), (CARRY_ROWS - buf.shape[1], 0), (0, 0)))


def kernel(x_prompt, x_sample, cache_conv_a, state_ssm_re, state_ssm_im, cache_ffn_conv,
           meta_tokens, norm_mix_g, w_in, conv_a_w, ssm_lambda_re, ssm_lambda_im, ssm_log_dt,
           ssm_b_re, ssm_b_im, ssm_c_re, ssm_c_im, ssm_d, glu_w, glu_b, proj_a, proj_b,
           w_out, norm_ffn_g, w_up, ffn_conv_w, ffn_conv_b, w_down, norm_final_g):
    batch, seq, _ = x_prompt.shape
    dec_batch, dec_seq, _ = x_sample.shape
    assert dec_seq == CHUNK and seq % CHUNK == 0 and seq % ROW_TILE == 0 and seq % FFN_ROW_TILE == 0
    n_chunks = seq // CHUNK
    assert batch * n_chunks == 256 and dec_batch == 8

    w_a = w_in[0, :, :3 * W_A].astype(BF16)
    w_ug = jnp.concatenate([w_in[0, :, 3 * W_A + W_B:], w_in[0, :, 3 * W_A:3 * W_A + W_B]],
                           axis=1).astype(BF16)
    glu_w16, proj_a16, proj_b16 = glu_w[0].astype(BF16), proj_a[0].astype(BF16), proj_b[0].astype(BF16)
    w_out16, w_up16, w_down16 = w_out[0].astype(BF16), w_up[0].astype(BF16), w_down[0].astype(BF16)
    g_mix, g_ffn, g_fin = norm_mix_g[0][None], norm_ffn_g[0][None], norm_final_g[None]
    glu_b2, ffn_b2 = glu_b[0][None], ffn_conv_b[0][None]

    n_s = dec_batch * dec_seq
    rows_s = n_s + N_META
    x_s = jnp.concatenate([x_sample.reshape(n_s, D_MODEL), meta_tokens], axis=0)
    segs_s = tuple((b * dec_seq, dec_seq) for b in range(dec_batch)) + ((n_s, N_META),)
    x_p = x_prompt.reshape(batch * seq, D_MODEL)
    segs_p = ((0, ROW_TILE),)
    chain = seq // ROW_TILE

    zero_hist = lambda c: jnp.zeros((1, CARRY_ROWS, c), F32)
    init_a_s = jnp.concatenate([_history_block(cache_conv_a[0]), zero_hist(W_A)], axis=0)
    hn_s, outa_s, ca_s = _mixer_a(x_s, g_mix, w_a, conv_a_w[0], init_a_s,
                                  tm=rows_s, segs=segs_s, chain_tiles=0)
    hn_p, outa_p, ca_p = _mixer_a(x_p, g_mix, w_a, conv_a_w[0], ca_s[dec_batch:],
                                  tm=ROW_TILE, segs=segs_p, chain_tiles=chain)
    gates_s, u_s = _gates_u(hn_s, w_ug, tm=rows_s)
    gates_p, u_p = _gates_u(hn_p, w_ug, tm=ROW_TILE)

    dup = lambda v: jnp.concatenate([v, v], axis=-1)[:, None, :]
    lre, lim = ssm_lambda_re[0], ssm_lambda_im[0]
    ldt = jnp.broadcast_to(ssm_log_dt[0][:, None], (SSM_G, SSM_P))
    bt_re, bt_im = ssm_b_re[0].swapaxes(1, 2), ssm_b_im[0].swapaxes(1, 2)
    bta = jnp.concatenate([bt_re, bt_im], axis=-1)
    btb = jnp.concatenate([-bt_im, bt_re], axis=-1)
    ddiag = ssm_d[0][:, :, None] * jnp.eye(SSM_H, dtype=F32)
    h0 = jnp.concatenate([state_ssm_re[0], state_ssm_im[0]], axis=-1).swapaxes(0, 1)
    yb_p, yb_s, h_fin = _ssm(u_p, u_s, lre[:, :, None], lim[:, :, None], ldt[:, :, None],
                             dup(lre), dup(lim), dup(ldt), bta, btb,
                             ssm_c_re[0].swapaxes(1, 2), ssm_c_im[0].swapaxes(1, 2), ddiag, h0,
                             n_prompt_chunks=batch * n_chunks, n_sample_chunks=dec_batch)

    x1_s = _merge(yb_s, outa_s, gates_s, x_s, glu_w16, glu_b2, proj_a16, proj_b16, w_out16, tm=rows_s)
    x1_p = _merge(yb_p, outa_p, gates_p, x_p, glu_w16, glu_b2, proj_a16, proj_b16, w_out16, tm=ROW_TILE)

    hist_f = _history_block(cache_ffn_conv[0])
    init_g_s = jnp.concatenate([hist_f[:, :, :D_FF], zero_hist(D_FF)], axis=0)
    init_v_s = jnp.concatenate([hist_f[:, :, D_FF:], zero_hist(D_FF)], axis=0)
    y_s, cg_s, cv_s = _ffn(x1_s, g_ffn, w_up16, ffn_conv_w[0], ffn_b2, w_down16, g_fin,
                           init_g_s, init_v_s, tm=rows_s, segs=segs_s, chain_tiles=0)
    y_p, cg_p, cv_p = _ffn(x1_p, g_ffn, w_up16, ffn_conv_w[0], ffn_b2, w_down16, g_fin,
                           cg_s[dec_batch:], cv_s[dec_batch:], tm=FFN_ROW_TILE,
                           segs=((0, FFN_ROW_TILE),), chain_tiles=seq // FFN_ROW_TILE)

    hist = lambda c: c[:, CARRY_ROWS - 2:][None]
    ffn_hist = lambda cg, cv: hist(jnp.concatenate([cg, cv], axis=-1))
    state = lambda h: (h[:, :, :SSM_P].swapaxes(0, 1)[None], h[:, :, SSM_P:].swapaxes(0, 1)[None])
    p_re, p_im = state(h_fin[:, 0:batch])
    s_re, s_im = state(h_fin[:, 8:8 + dec_batch])
    return (y_p.reshape(batch, seq, D_MODEL), y_s[:n_s].reshape(dec_batch, dec_seq, D_MODEL),
            hist(ca_p), p_re, p_im, ffn_hist(cg_p, cv_p),
            hist(ca_s[:dec_batch]), s_re, s_im, ffn_hist(cg_s[:dec_batch], cv_s[:dec_batch]))
```

```python
import functools

import jax
import jax.numpy as jnp
from jax import lax
from jax.experimental import pallas as pl
from jax.experimental.pallas import tpu as pltpu

D_MODEL = 2048
W_A = 1024
W_B = 1024
SSM_H = 16
SSM_G = 64
SSM_P = 64
D_FF = 5632
CHUNK = 64
N_META = 16
EPS = 1e-6

BF16 = jnp.bfloat16
F32 = jnp.float32
HIGHEST = lax.Precision.HIGHEST

CARRY_ROWS = 8
ROW_TILE = 512
FFN_ROW_TILE = 512
FF_TILE = 512
MXU_COLS = 256
GATE_ROWS = 128
SLAB = 128
SLAB_GROUPS = SLAB // SSM_H
SSM_ROWS = 272
VMEM_LIMIT = 56 * 1024 * 1024


def _dot(a, b):
    return jnp.dot(a, b, preferred_element_type=F32)


def _dot_hi(a, b):
    return jnp.dot(a, b, preferred_element_type=F32, precision=HIGHEST)


def _rmsnorm(x, g):
    ms = jnp.mean(x * x, axis=-1, keepdims=True)
    return (x * lax.rsqrt(ms + EPS)) * g


def _params(n_axes):
    return pltpu.CompilerParams(
        dimension_semantics=("arbitrary",) * n_axes, vmem_limit_bytes=VMEM_LIMIT)


def _resident(shape):
    return pl.BlockSpec(shape, lambda *_: (0,) * len(shape), pipeline_mode=pl.Buffered(1))


def _stash_rows(ext_ref, v, segs, cols=slice(None)):
    for k, (s, n) in enumerate(segs):
        base = s + CARRY_ROWS * (k + 1)
        ext_ref[base:base + n, cols] = v[s:s + n]


def _conv_prepare(ext_ref, init_ref, cout_ref, carry_ref, segs, chain_tiles):
    for k, (s, n) in enumerate(segs):
        base = s + CARRY_ROWS * k
        ext_ref[base:base + CARRY_ROWS, :] = carry_ref[...] if chain_tiles else init_ref[k]
        last = ext_ref[base + n:base + n + CARRY_ROWS, :]
        cout_ref[k] = last
        if chain_tiles:
            carry_ref[...] = last


def _conv_piece(ext_ref, w_ref, base, rows, cols=slice(None)):
    r0, r1 = rows
    t2 = ext_ref[base + 6 + r0:base + 6 + r1, cols]
    t1 = ext_ref[base + 7 + r0:base + 7 + r1, cols]
    t0 = ext_ref[base + 8 + r0:base + 8 + r1, cols]
    return t2 * w_ref[0:1, cols] + t1 * w_ref[1:2, cols] + t0 * w_ref[2:3, cols]


def _start_stream(init_ref, carry_ref, tile, chain_tiles):
    if chain_tiles:
        @pl.when(tile % chain_tiles == 0)
        def _():
            carry_ref[...] = init_ref[0]


def _mixer_a_kernel(x_ref, g_ref, w_ref, cw_ref, init_ref, hn_ref, outa_ref, cout_ref,
                    ext_ref, carry_ref, *, segs, chain_tiles):
    _start_stream(init_ref, carry_ref, pl.program_id(0), chain_tiles)
    hn = _rmsnorm(x_ref[...], g_ref[...]).astype(BF16)
    hn_ref[...] = hn
    zb = _dot(hn, w_ref[:, 0:W_A])
    zc = _dot(hn, w_ref[:, W_A:2 * W_A])
    zh = _dot(hn, w_ref[:, 2 * W_A:3 * W_A])
    _stash_rows(ext_ref, zc * zh, segs)
    _conv_prepare(ext_ref, init_ref, cout_ref, carry_ref, segs, chain_tiles)
    for k, (s, n) in enumerate(segs):
        conv = _conv_piece(ext_ref, cw_ref, s + CARRY_ROWS * k, (0, n))
        outa_ref[s:s + n, :] = (zb[s:s + n] * conv).astype(BF16)


def _mixer_a(x, g, w_a, conv_w, init, *, tm, segs, chain_tiles):
    rows = x.shape[0]
    n_tiles = rows // tm
    nseg = len(segs)
    n_out_seq = n_tiles // chain_tiles if chain_tiles else 1
    seq = (lambda i: (i // chain_tiles, 0, 0)) if chain_tiles else (lambda i: (0, 0, 0))
    return pl.pallas_call(
        functools.partial(_mixer_a_kernel, segs=segs, chain_tiles=chain_tiles),
        grid=(n_tiles,),
        in_specs=[
            pl.BlockSpec((tm, D_MODEL), lambda i: (i, 0)),
            _resident((1, D_MODEL)),
            _resident((D_MODEL, 3 * W_A)),
            _resident((3, W_A)),
            pl.BlockSpec((nseg, CARRY_ROWS, W_A), lambda i: (0, 0, 0)),
        ],
        out_specs=[
            pl.BlockSpec((tm, D_MODEL), lambda i: (i, 0)),
            pl.BlockSpec((tm, W_A), lambda i: (i, 0)),
            pl.BlockSpec((nseg, CARRY_ROWS, W_A), seq),
        ],
        out_shape=[
            jax.ShapeDtypeStruct((rows, D_MODEL), BF16),
            jax.ShapeDtypeStruct((rows, W_A), BF16),
            jax.ShapeDtypeStruct((n_out_seq * nseg, CARRY_ROWS, W_A), F32),
        ],
        scratch_shapes=[
            pltpu.VMEM((tm + CARRY_ROWS * nseg, W_A), F32),
            pltpu.VMEM((CARRY_ROWS, W_A), F32),
        ],
        compiler_params=_params(1),
        name="mixer_a",
    )(x, g, w_a, conv_w, init)


def _transpose_blocks(x):
    rows = x.shape[0]
    shape3 = (rows // 8, 8, SLAB)
    blk = lax.broadcasted_iota(jnp.int32, shape3, 2) >> 4
    x = pltpu.roll(x, 0, 1, stride=SSM_H, stride_axis=0)
    x3 = x.reshape(shape3)
    for d in (4, 2, 1):
        x3 = jnp.where((blk & d) != 0, pltpu.roll(x3, 8 - d, 1), x3)
    x3 = pltpu.roll(x3.reshape(rows, SLAB), 0, 1, stride=SSM_H, stride_axis=0).reshape(shape3)
    r = lax.broadcasted_iota(jnp.int32, shape3, 1) & 3
    x3 = jnp.where(r == 0, x3, jnp.where(r == 1, pltpu.roll(x3, 2, 1),
                                        jnp.where(r == 2, pltpu.roll(x3, 4, 1), pltpu.roll(x3, 6, 1))))
    return x3.reshape(rows, SLAB)


def _swap_outer_sublane(a):
    p = lax.broadcasted_iota(jnp.int32, a.shape, 0)
    q = lax.broadcasted_iota(jnp.int32, a.shape, 2)
    for d in (4, 2, 1):
        p_hi, q_hi = (p & d) != 0, (q & d) != 0
        up = pltpu.roll(jnp.roll(a, d, axis=0), 8 - d, 2)
        dn = pltpu.roll(jnp.roll(a, -d, axis=0), d, 2)
        a = jnp.where(jnp.logical_and(p_hi, jnp.logical_not(q_hi)), up,
                      jnp.where(jnp.logical_and(jnp.logical_not(p_hi), q_hi), dn, a))
    return a


OCTETS = CHUNK // 8
TILE_CHUNKS = 8


def _to_grouped(x, dst_ref, lanes):
    xs = _transpose_blocks(x)
    n_c = x.shape[0] // CHUNK
    for c0 in range(0, n_c - n_c % TILE_CHUNKS, TILE_CHUNKS):
        tile = xs[c0 * CHUNK:(c0 + TILE_CHUNKS) * CHUNK].reshape(TILE_CHUNKS, OCTETS, 8, SLAB)
        dst_ref[:, :, c0:c0 + TILE_CHUNKS, lanes] = _swap_outer_sublane(tile)
    for c in range(n_c - n_c % TILE_CHUNKS, n_c):
        for i in range(OCTETS):
            for s in range(8):
                r = c * CHUNK + 8 * i + s
                dst_ref[s, i, c:c + 1, lanes] = xs[r:r + 1]


def _from_grouped(src_ref, lanes, n_c, tail_ref):
    parts = []
    for c0 in range(0, n_c - n_c % TILE_CHUNKS, TILE_CHUNKS):
        tile = _swap_outer_sublane(src_ref[:, :, c0:c0 + TILE_CHUNKS, lanes])
        parts.append(tile.reshape(TILE_CHUNKS * CHUNK, SLAB))
    for k, c in enumerate(range(n_c - n_c % TILE_CHUNKS, n_c)):
        for i in range(OCTETS):
            for s in range(8):
                r = k * CHUNK + 8 * i + s
                tail_ref[r:r + 1, lanes] = src_ref[s, i, c:c + 1, lanes]
    if n_c % TILE_CHUNKS:
        parts.append(tail_ref[0:(n_c % TILE_CHUNKS) * CHUNK, lanes])
    xs = parts[0] if len(parts) == 1 else jnp.concatenate(parts, axis=0)
    return _transpose_blocks(xs)


GATE_COLS = 2 * D_MODEL


def _gates_u_kernel(hn_ref, w_ref, gates_ref, u_ref, *, n_chunk_rows):
    hn = hn_ref[...]
    u = _dot(hn, w_ref[:, GATE_COLS:])
    n_c = hn.shape[0] // CHUNK
    if n_chunk_rows > n_c:
        u_ref[:, :, n_c:, :] = jnp.zeros((8, OCTETS, n_chunk_rows - n_c, W_B), F32)
    n_gate = GATE_COLS // 1024
    slabs_per_gate = W_B // SLAB // n_gate
    for k in range(n_gate):
        cols = slice(k * 1024, (k + 1) * 1024)
        gates_ref[:, cols] = jax.nn.sigmoid(_dot(hn, w_ref[:, cols])).astype(BF16)
        for j in range(k * slabs_per_gate, (k + 1) * slabs_per_gate):
            lanes = slice(j * SLAB, (j + 1) * SLAB)
            _to_grouped(u[:, lanes], u_ref, lanes)


def _gates_u(hn, w_ug, *, tm):
    rows = hn.shape[0]
    n_tiles = rows // tm
    n_c = tm // CHUNK
    n_chunk_rows = n_c if n_tiles > 1 else -(-n_c // 16) * 16
    return pl.pallas_call(
        functools.partial(_gates_u_kernel, n_chunk_rows=n_chunk_rows),
        grid=(n_tiles,),
        in_specs=[
            pl.BlockSpec((tm, D_MODEL), lambda i: (i, 0)),
            _resident((D_MODEL, GATE_COLS + W_B)),
        ],
        out_specs=[pl.BlockSpec((tm, GATE_COLS), lambda i: (i, 0)),
                   pl.BlockSpec((8, OCTETS, n_chunk_rows, W_B), lambda i: (0, 0, i, 0))],
        out_shape=[jax.ShapeDtypeStruct((rows, GATE_COLS), BF16),
                   jax.ShapeDtypeStruct((8, OCTETS, n_tiles * n_chunk_rows, W_B), F32)],
        compiler_params=_params(1),
        name="gates_u",
    )(hn, w_ug)


def _ssm_group(u, lre_c, lim_c, ldt_c, lre_r, lim_r, ldt_r, bta, btb, ctr, cti, ddiag, h0,
               sel_lag, sel_ch, sel_row, t_ref, xs_ref):
    lane128 = lax.broadcasted_iota(jnp.int32, (1, 128), 1)
    sgn_conj = jnp.where(lane128 < SSM_P, 1.0, -1.0).astype(F32)
    sgn_mul = -sgn_conj

    def swap(h):
        return pltpu.roll(h, SSM_P, axis=1)

    def cmul(ar, ai, h):
        return ar * h + (ai * sgn_mul) * swap(h)

    dt_c = jnp.exp(ldt_c)
    k_row = lax.broadcasted_iota(jnp.int32, (1, 128), 1).astype(F32)
    mag = jnp.exp((lre_c * dt_c) * k_row)
    ang = (lim_c * dt_c) * k_row
    apt_re = mag * jnp.cos(ang)
    apt_im = mag * jnp.sin(ang)
    ae_re = _dot_hi(apt_re, sel_lag)
    ae_im = _dot_hi(apt_im, sel_lag)
    cte_re = _dot_hi(ctr, sel_ch)
    cte_im = _dot_hi(cti, sel_ch)
    ca_re = cte_re * ae_re - cte_im * ae_im
    ca_im = cte_re * ae_im + cte_im * ae_re
    a1_re = apt_re[:, 1:2]
    a1_im = apt_im[:, 1:2]
    ca1_re = ca_re * a1_re - ca_im * a1_im
    ca1_im = ca_re * a1_im + ca_im * a1_re

    dt_r = jnp.exp(ldt_r)
    xr = lre_r * dt_r
    xi = lim_r * dt_r
    a_re = jnp.exp(xr) * jnp.cos(xi)
    a_im = jnp.exp(xr) * jnp.sin(xi)
    den = lre_r * lre_r + lim_r * lim_r
    nr = a_re - 1.0
    cr = (nr * lre_r + a_im * lim_r) / den
    ci = (a_im * lre_r - nr * lim_r) / den
    bb_a = bta * cr + btb * ci
    bb_b = btb * cr - bta * ci

    k2 = _dot_hi(bb_a * sgn_conj, jnp.concatenate([ca_re, ca_im], axis=0))
    lane1k = lax.broadcasted_iota(jnp.int32, (SSM_H, CHUNK * SSM_H), 1)
    k2 = k2 + jnp.where(lane1k < SSM_H, _dot_hi(ddiag, sel_ch), 0.0)
    k2z = jnp.concatenate([k2, jnp.zeros_like(k2)], axis=1)
    for r in range(8):
        kr = pltpu.roll(k2z, SSM_H * r, axis=1) if r else k2z
        for q in range(CHUNK // 8):
            m = 8 * q + r
            rows = slice(SSM_H * m, SSM_H * (m + 1))
            if q:
                t_ref[rows, 0:128 * q] = jnp.zeros((SSM_H, 128 * q), BF16)
            t_ref[rows, 128 * q:] = kr[:, 0:CHUNK * SSM_H - 128 * q].astype(BF16)

    e_col = (CHUNK - 1 - lax.broadcasted_iota(jnp.int32, (CHUNK, 1), 0)).astype(F32)
    apm = jnp.exp(e_col * xr)
    apa = e_col * xi
    apx_re = _dot_hi(sel_row, apm * jnp.cos(apa))
    apx_im = _dot_hi(sel_row, apm * jnp.sin(apa))
    tile_rows = lambda b: jnp.broadcast_to(b[None], (CHUNK, SSM_H, 128)).reshape(CHUNK * SSM_H, 128)
    bcat = (apx_re * tile_rows(bb_a) + apx_im * tile_rows(bb_b)).astype(BF16)
    ccat = jnp.concatenate([ca1_re, -ca1_im], axis=0).astype(BF16)

    s = _dot(u, bcat)
    al_re = jnp.exp(CHUNK * xr) * jnp.cos(CHUNK * xi)
    al_im = jnp.exp(CHUNK * xr) * jnp.sin(CHUNK * xi)

    h_meta = s[264:265]
    h_sample = cmul(al_re, al_im, h0) + s[256:264]

    n_p, n_c = 256, 128
    chunk = lax.broadcasted_iota(jnp.int32, (n_p, 128), 0) & (n_c - 1)
    x = s[0:n_p] + jnp.where(chunk == 0, cmul(al_re, al_im, h_meta), 0.0)
    head = n_c
    xs_ref[0:head, :] = jnp.zeros((head, 128), F32)
    ar, ai = al_re, al_im
    for step in range(7):
        sh = 1 << step
        xs_ref[head:head + n_p, :] = x
        x = x + jnp.where(chunk >= sh, cmul(ar, ai, xs_ref[head - sh:head + n_p - sh, :]), 0.0)
        ar, ai = ar * ar - ai * ai, 2.0 * ar * ai
    xs_ref[head:head + n_p, :] = x
    h_in_prompt = jnp.where(chunk == 0, h_meta, xs_ref[head - 1:head + n_p - 1, :])
    h_in = jnp.concatenate([h_in_prompt, h0, jnp.zeros((SSM_ROWS - n_p - 8, 128), F32)], axis=0)

    y = _dot(u, t_ref[...]) + _dot(h_in.astype(BF16), ccat)
    h_fin = jnp.concatenate([x[n_c - 1:n_c], x[n_p - 1:n_p], jnp.zeros((6, 128), F32), h_sample], axis=0)
    return y, h_fin


def _ssm_kernel(up_ref, us_ref, lre_c_ref, lim_c_ref, ldt_c_ref, lre_r_ref, lim_r_ref, ldt_r_ref,
                bta_ref, btb_ref, ctr_ref, cti_ref, dd_ref, h0_ref, yp_ref, ys_ref, hfin_ref,
                t_ref, xs_ref):
    kl = CHUNK * SSM_H
    lag_of_lane = lax.broadcasted_iota(jnp.int32, (128, kl), 1) >> 4
    sel_lag = (lax.broadcasted_iota(jnp.int32, (128, kl), 0) == lag_of_lane).astype(F32)
    ch_of_lane = lax.broadcasted_iota(jnp.int32, (SSM_H, kl), 1) & (SSM_H - 1)
    sel_ch = (lax.broadcasted_iota(jnp.int32, (SSM_H, kl), 0) == ch_of_lane).astype(F32)
    lag_of_row = lax.broadcasted_iota(jnp.int32, (kl, CHUNK), 0) >> 4
    sel_row = (lax.broadcasted_iota(jnp.int32, (kl, CHUNK), 1) == lag_of_row).astype(F32)
    n_p = up_ref.shape[2]
    for s in range(SLAB_GROUPS):
        u_p = jnp.concatenate([up_ref[s, i] for i in range(OCTETS)], axis=1)
        u_s = jnp.concatenate([us_ref[s, i] for i in range(OCTETS)], axis=1)
        u = jnp.concatenate([u_p, u_s], axis=0).astype(BF16)
        y, h_fin = _ssm_group(
            u, lre_c_ref[s], lim_c_ref[s], ldt_c_ref[s], lre_r_ref[s], lim_r_ref[s],
            ldt_r_ref[s], bta_ref[s], btb_ref[s], ctr_ref[s], cti_ref[s], dd_ref[s], h0_ref[s],
            sel_lag, sel_ch, sel_row, t_ref.at[s % 2], xs_ref)
        hfin_ref[s] = h_fin
        for i in range(OCTETS):
            lanes = slice(SLAB * i, SLAB * (i + 1))
            yp_ref[s, i] = y[0:n_p, lanes]
            ys_ref[s, i] = y[n_p:, lanes]


def _ssm(u_p, u_s, lre_c, lim_c, ldt_c, lre_r, lim_r, ldt_r, bta, btb, ctr, cti, ddiag, h0):
    gb = SLAB_GROUPS
    per_group = lambda *tail: pl.BlockSpec((gb,) + tail, lambda j: (j,) + (0,) * len(tail))
    slab = lambda a: pl.BlockSpec(a.shape[:3] + (SLAB,), lambda j: (0, 0, 0, j))
    assert u_p.shape[2] + u_s.shape[2] == SSM_ROWS
    return pl.pallas_call(
        _ssm_kernel,
        grid=(SSM_G // gb,),
        in_specs=[
            slab(u_p), slab(u_s),
            per_group(SSM_P, 1), per_group(SSM_P, 1), per_group(SSM_P, 1),
            per_group(1, 128), per_group(1, 128), per_group(1, 128),
            per_group(SSM_H, 128), per_group(SSM_H, 128),
            per_group(SSM_P, SSM_H), per_group(SSM_P, SSM_H),
            per_group(SSM_H, SSM_H),
            per_group(8, 128),
        ],
        out_specs=[slab(u_p), slab(u_s), per_group(16, 128)],
        out_shape=[
            jax.ShapeDtypeStruct(u_p.shape, F32),
            jax.ShapeDtypeStruct(u_s.shape, F32),
            jax.ShapeDtypeStruct((SSM_G, 16, 128), F32),
        ],
        scratch_shapes=[
            pltpu.VMEM((2, CHUNK * SSM_H, CHUNK * SSM_H), BF16),
            pltpu.VMEM((128 + 256, 128), F32),
        ],
        compiler_params=_params(1),
        name="ssm",
    )(u_p, u_s, lre_c, lim_c, ldt_c, lre_r, lim_r, ldt_r, bta, btb, ctr, cti, ddiag, h0)


def _merge_kernel(y_ref, outa_ref, sga_ref, sgb_ref, x_ref, gluw_ref, glub_ref, pa_ref,
                  pb_ref, wo_ref, x1_ref, yt_ref, merged_ref, tail_ref):
    tm = x_ref.shape[0]
    n_c = tm // CHUNK
    n_piece = 4
    width = D_MODEL // n_piece
    slabs_per_piece = W_B // SLAB // n_piece
    outa = outa_ref[...]
    proj = []
    for k in range(n_piece):
        proj.append(_dot(outa, pa_ref[:, k * width:(k + 1) * width]))
        for j in range(k * slabs_per_piece, (k + 1) * slabs_per_piece):
            lanes = slice(j * SLAB, (j + 1) * SLAB)
            yt_ref[:, lanes] = jax.nn.gelu(_from_grouped(y_ref, lanes, n_c, tail_ref))
    yb = yt_ref[...]
    glu = _dot(yb.astype(BF16), gluw_ref[...]) + glub_ref[...]
    out_b = (yb * jax.nn.sigmoid(glu)).astype(BF16)
    for k in range(n_piece):
        cols = slice(k * width, (k + 1) * width)
        merged_ref[:, cols] = (sga_ref[:, cols].astype(F32) * proj[k]
                               + sgb_ref[:, cols].astype(F32) * _dot(out_b, pb_ref[:, cols])).astype(BF16)
    x1_ref[...] = x_ref[...] + _dot(merged_ref[...], wo_ref[...])


def _merge(y_grouped, outa, ug, x, glu_w, glu_b, proj_a, proj_b, w_out, *, tm):
    rows = x.shape[0]
    n_tiles = rows // tm
    n_chunk_rows = y_grouped.shape[2] // n_tiles
    return pl.pallas_call(
        _merge_kernel,
        grid=(n_tiles,),
        in_specs=[
            pl.BlockSpec((8, OCTETS, n_chunk_rows, W_B), lambda i: (0, 0, i, 0)),
            pl.BlockSpec((tm, W_A), lambda i: (i, 0)),
            pl.BlockSpec((tm, D_MODEL), lambda i: (i, 0)),
            pl.BlockSpec((tm, D_MODEL), lambda i: (i, 1)),
            pl.BlockSpec((tm, D_MODEL), lambda i: (i, 0)),
            _resident((W_B, W_B)),
            _resident((1, W_B)),
            _resident((W_A, D_MODEL)),
            _resident((W_B, D_MODEL)),
            _resident((D_MODEL, D_MODEL)),
        ],
        out_specs=pl.BlockSpec((tm, D_MODEL), lambda i: (i, 0)),
        out_shape=jax.ShapeDtypeStruct((rows, D_MODEL), F32),
        scratch_shapes=[
            pltpu.VMEM((tm, W_B), F32),
            pltpu.VMEM((tm, D_MODEL), BF16),
            pltpu.VMEM((CHUNK * max(tm // CHUNK % TILE_CHUNKS, 1), W_B), F32),
        ],
        compiler_params=_params(1),
        name="merge",
    )(y_grouped, outa, ug, ug, x, glu_w, glu_b, proj_a, proj_b, w_out)


def _ffn_kernel(x1_ref, g_ref, wug_ref, wuv_ref, cwg_ref, cwv_ref, bg_ref, bv_ref, wd_ref,
                gfin_ref, initg_ref, initv_ref, y_ref, coutg_ref, coutv_ref,
                hn_ref, extg0_ref, extg1_ref, extv0_ref, extv1_ref, act0_ref, act1_ref,
                carryg_ref, carryv_ref, *, segs, chain_tiles, n_f, n_items):
    t = pl.program_id(0)
    item1 = jnp.clip(t - 1, 0, n_items - 1)
    item2 = jnp.clip(t - 2, 0, n_items - 1)
    f1, tile1 = item1 % n_f, item1 // n_f
    f2 = item2 % n_f
    draining = t >= 2

    @pl.when(t == 0)
    def _():
        extg1_ref[...] = jnp.zeros(extg1_ref.shape, F32)
        extv1_ref[...] = jnp.zeros(extv1_ref.shape, F32)
        act0_ref[...] = jnp.zeros(act0_ref.shape, BF16)
        y_ref[...] = jnp.zeros(y_ref.shape, F32)

    @pl.when(jnp.minimum(t, n_items - 1) % n_f == 0)
    def _():
        hn_ref[...] = _rmsnorm(x1_ref[...], g_ref[...]).astype(BF16)

    @pl.when(jnp.logical_and(f2 == 0, draining))
    def _():
        y_ref[...] = x1_ref[...]

    _start_stream(initg_ref, carryg_ref.at[f1], tile1, chain_tiles)
    _start_stream(initv_ref, carryv_ref.at[f1], tile1, chain_tiles)

    bufs = ((extg0_ref, extv0_ref, act0_ref), (extg1_ref, extv1_ref, act1_ref))
    for par in (0, 1):
        @pl.when(t % 2 == par)
        def _(par=par):
            up_g, up_v, act_in = bufs[par]
            conv_src_g, conv_src_v, act_out = bufs[1 - par]

            def up_piece(dst_ref, w_ref, cols):
                return lambda: _stash_rows(dst_ref, _dot(hn_ref[...], w_ref[:, cols]), segs, cols)

            def down_piece(cols):
                def run():
                    y_ref[:, cols] += _dot(act_in[...], wd_ref[:, cols])
                return run

            def gate_piece(k, s, rows, cols):
                def run():
                    base = s + CARRY_ROWS * k
                    g = _conv_piece(conv_src_g, cwg_ref, base, rows, cols) + bg_ref[:, cols]
                    v = _conv_piece(conv_src_v, cwv_ref, base, rows, cols) + bv_ref[:, cols]
                    act_out[s + rows[0]:s + rows[1], cols] = (jax.nn.silu(g) * v).astype(BF16)
                return run

            col_slices = lambda width, step: [slice(c, c + step) for c in range(0, width, step)]
            mxu = ([(up_piece(up_g, wug_ref, c), 4) for c in col_slices(FF_TILE, MXU_COLS)]
                   + [(up_piece(up_v, wuv_ref, c), 4) for c in col_slices(FF_TILE, MXU_COLS)]
                   + [(down_piece(c), 1) for c in col_slices(D_MODEL, MXU_COLS)])
            vpu = [gate_piece(k, s, (r, min(r + GATE_ROWS, n)), c)
                   for c in col_slices(FF_TILE, 128)
                   for k, (s, n) in enumerate(segs) for r in range(0, n, GATE_ROWS)]

            _conv_prepare(conv_src_g, initg_ref, coutg_ref.at[:, f1], carryg_ref.at[f1], segs, chain_tiles)
            _conv_prepare(conv_src_v, initv_ref, coutv_ref.at[:, f1], carryv_ref.at[f1], segs, chain_tiles)
            total = sum(weight for _, weight in mxu)
            done, issued = 0, 0
            for run, weight in mxu:
                run()
                done += weight
                while issued < len(vpu) and issued * total < done * len(vpu):
                    vpu[issued]()
                    issued += 1

    @pl.when(jnp.logical_and(f2 == n_f - 1, draining))
    def _():
        y_ref[...] = _rmsnorm(y_ref[...], gfin_ref[...])


def _ffn(x1, g, w_up, conv_w, conv_b, w_down, g_fin, init_g, init_v, *, tm, segs, chain_tiles):
    rows = x1.shape[0]
    n_tiles = rows // tm
    n_f = D_FF // FF_TILE
    n_items = n_tiles * n_f
    nseg = len(segs)
    n_out_seq = n_tiles // chain_tiles if chain_tiles else 1
    item = lambda t, lag: jnp.clip(t - lag, 0, n_items - 1)
    ff = lambda lag: (lambda t: item(t, lag) % n_f)
    tile = lambda lag: (lambda t: item(t, lag) // n_f)
    seq1 = lambda t: tile(1)(t) // chain_tiles if chain_tiles else 0
    carry_block = (nseg, CARRY_ROWS, FF_TILE)
    cout_block = (nseg, n_f, CARRY_ROWS, FF_TILE)
    cout_shape = jax.ShapeDtypeStruct((n_out_seq * nseg, n_f, CARRY_ROWS, FF_TILE), F32)
    ext_shape = (tm + CARRY_ROWS * nseg, FF_TILE)
    y, cout_g, cout_v = pl.pallas_call(
        functools.partial(_ffn_kernel, segs=segs, chain_tiles=chain_tiles, n_f=n_f, n_items=n_items),
        grid=(n_items + 2,),
        in_specs=[
            pl.BlockSpec((tm, D_MODEL), lambda t: (tile(0)(t), 0)),
            pl.BlockSpec((1, D_MODEL), lambda t: (0, 0)),
            pl.BlockSpec((D_MODEL, FF_TILE), lambda t: (0, ff(0)(t))),
            pl.BlockSpec((D_MODEL, FF_TILE), lambda t: (0, n_f + ff(0)(t))),
            pl.BlockSpec((3, FF_TILE), lambda t: (0, ff(1)(t))),
            pl.BlockSpec((3, FF_TILE), lambda t: (0, n_f + ff(1)(t))),
            pl.BlockSpec((1, FF_TILE), lambda t: (0, ff(1)(t))),
            pl.BlockSpec((1, FF_TILE), lambda t: (0, n_f + ff(1)(t))),
            pl.BlockSpec((FF_TILE, D_MODEL), lambda t: (ff(2)(t), 0)),
            pl.BlockSpec((1, D_MODEL), lambda t: (0, 0)),
            pl.BlockSpec(carry_block, lambda t: (0, 0, ff(1)(t))),
            pl.BlockSpec(carry_block, lambda t: (0, 0, ff(1)(t))),
        ],
        out_specs=[
            pl.BlockSpec((tm, D_MODEL), lambda t: (tile(2)(t), 0)),
            pl.BlockSpec(cout_block, lambda t: (seq1(t), 0, 0, 0)),
            pl.BlockSpec(cout_block, lambda t: (seq1(t), 0, 0, 0)),
        ],
        out_shape=[jax.ShapeDtypeStruct((rows, D_MODEL), F32), cout_shape, cout_shape],
        scratch_shapes=[
            pltpu.VMEM((tm, D_MODEL), BF16),
            pltpu.VMEM(ext_shape, F32), pltpu.VMEM(ext_shape, F32),
            pltpu.VMEM(ext_shape, F32), pltpu.VMEM(ext_shape, F32),
            pltpu.VMEM((tm, FF_TILE), BF16), pltpu.VMEM((tm, FF_TILE), BF16),
            pltpu.VMEM((n_f, CARRY_ROWS, FF_TILE), F32),
            pltpu.VMEM((n_f, CARRY_ROWS, FF_TILE), F32),
        ],
        compiler_params=_params(1),
        name="ffn",
    )(x1, g, w_up, w_up, conv_w, conv_w, conv_b, conv_b, w_down, g_fin, init_g, init_v)
    widen = lambda c: c.transpose(0, 2, 1, 3).reshape(c.shape[0], CARRY_ROWS, D_FF)
    return y, widen(cout_g), widen(cout_v)


def _history_block(buf):
    return jnp.pad(buf, ((0, 0), (CARRY_ROWS - buf.shape[1], 0), (0, 0)))


def kernel(x_prompt, x_sample, cache_conv_a, state_ssm_re, state_ssm_im, cache_ffn_conv,
           meta_tokens, norm_mix_g, w_in, conv_a_w, ssm_lambda_re, ssm_lambda_im, ssm_log_dt,
           ssm_b_re, ssm_b_im, ssm_c_re, ssm_c_im, ssm_d, glu_w, glu_b, proj_a, proj_b,
           w_out, norm_ffn_g, w_up, ffn_conv_w, ffn_conv_b, w_down, norm_final_g):
    batch, seq, _ = x_prompt.shape
    dec_batch, dec_seq, _ = x_sample.shape
    assert dec_seq == CHUNK and seq % CHUNK == 0 and seq % ROW_TILE == 0 and seq % FFN_ROW_TILE == 0
    n_chunks = seq // CHUNK
    assert batch * n_chunks == 256 and dec_batch == 8

    w_a = w_in[0, :, :3 * W_A].astype(BF16)
    w_ug = jnp.concatenate([w_in[0, :, 3 * W_A + W_B:], w_in[0, :, 3 * W_A:3 * W_A + W_B]],
                           axis=1).astype(BF16)
    glu_w16, proj_a16, proj_b16 = glu_w[0].astype(BF16), proj_a[0].astype(BF16), proj_b[0].astype(BF16)
    w_out16, w_up16, w_down16 = w_out[0].astype(BF16), w_up[0].astype(BF16), w_down[0].astype(BF16)
    g_mix, g_ffn, g_fin = norm_mix_g[0][None], norm_ffn_g[0][None], norm_final_g[None]
    glu_b2, ffn_b2 = glu_b[0][None], ffn_conv_b[0][None]

    n_s = dec_batch * dec_seq
    n_pad = CHUNK - N_META
    rows_s = n_s + CHUNK
    x_s = jnp.concatenate([x_sample.reshape(n_s, D_MODEL), jnp.zeros((n_pad, D_MODEL), F32), meta_tokens],
                          axis=0)
    segs_s = (tuple((b * dec_seq, dec_seq) for b in range(dec_batch))
              + ((n_s, n_pad), (n_s + n_pad, N_META)))
    meta_seg = dec_batch + 1
    x_p = x_prompt.reshape(batch * seq, D_MODEL)
    segs_p = ((0, ROW_TILE),)
    chain = seq // ROW_TILE

    zero_hist = lambda c: jnp.zeros((1, CARRY_ROWS, c), F32)
    init_a_s = jnp.concatenate([_history_block(cache_conv_a[0]), zero_hist(W_A), zero_hist(W_A)], axis=0)
    hn_s, outa_s, ca_s = _mixer_a(x_s, g_mix, w_a, conv_a_w[0], init_a_s,
                                  tm=rows_s, segs=segs_s, chain_tiles=0)
    hn_p, outa_p, ca_p = _mixer_a(x_p, g_mix, w_a, conv_a_w[0], ca_s[meta_seg:],
                                  tm=ROW_TILE, segs=segs_p, chain_tiles=chain)
    gates_s, u_s = _gates_u(hn_s, w_ug, tm=rows_s)
    gates_p, u_p = _gates_u(hn_p, w_ug, tm=ROW_TILE)

    dup = lambda v: jnp.concatenate([v, v], axis=-1)[:, None, :]
    lre, lim = ssm_lambda_re[0], ssm_lambda_im[0]
    ldt = jnp.broadcast_to(ssm_log_dt[0][:, None], (SSM_G, SSM_P))
    bt_re, bt_im = ssm_b_re[0].swapaxes(1, 2), ssm_b_im[0].swapaxes(1, 2)
    bta = jnp.concatenate([bt_re, bt_im], axis=-1)
    btb = jnp.concatenate([-bt_im, bt_re], axis=-1)
    ddiag = ssm_d[0][:, :, None] * jnp.eye(SSM_H, dtype=F32)
    h0 = jnp.concatenate([state_ssm_re[0], state_ssm_im[0]], axis=-1).swapaxes(0, 1)
    yb_p, yb_s, h_fin = _ssm(u_p, u_s, lre[:, :, None], lim[:, :, None], ldt[:, :, None],
                             dup(lre), dup(lim), dup(ldt), bta, btb,
                             ssm_c_re[0].swapaxes(1, 2), ssm_c_im[0].swapaxes(1, 2), ddiag, h0)

    x1_s = _merge(yb_s, outa_s, gates_s, x_s, glu_w16, glu_b2, proj_a16, proj_b16, w_out16, tm=rows_s)
    x1_p = _merge(yb_p, outa_p, gates_p, x_p, glu_w16, glu_b2, proj_a16, proj_b16, w_out16, tm=ROW_TILE)

    hist_f = _history_block(cache_ffn_conv[0])
    init_g_s = jnp.concatenate([hist_f[:, :, :D_FF], zero_hist(D_FF), zero_hist(D_FF)], axis=0)
    init_v_s = jnp.concatenate([hist_f[:, :, D_FF:], zero_hist(D_FF), zero_hist(D_FF)], axis=0)
    y_s, cg_s, cv_s = _ffn(x1_s, g_ffn, w_up16, ffn_conv_w[0], ffn_b2, w_down16, g_fin,
                           init_g_s, init_v_s, tm=rows_s, segs=segs_s, chain_tiles=0)
    y_p, cg_p, cv_p = _ffn(x1_p, g_ffn, w_up16, ffn_conv_w[0], ffn_b2, w_down16, g_fin,
                           cg_s[meta_seg:], cv_s[meta_seg:], tm=FFN_ROW_TILE,
                           segs=((0, FFN_ROW_TILE),), chain_tiles=seq // FFN_ROW_TILE)

    hist = lambda c: c[:, CARRY_ROWS - 2:][None]
    ffn_hist = lambda cg, cv: hist(jnp.concatenate([cg, cv], axis=-1))
    state = lambda h: (h[:, :, :SSM_P].swapaxes(0, 1)[None], h[:, :, SSM_P:].swapaxes(0, 1)[None])
    p_re, p_im = state(h_fin[:, 0:batch])
    s_re, s_im = state(h_fin[:, 8:8 + dec_batch])
    return (y_p.reshape(batch, seq, D_MODEL), y_s[:n_s].reshape(dec_batch, dec_seq, D_MODEL),
            hist(ca_p), p_re, p_im, ffn_hist(cg_p, cv_p),
            hist(ca_s[:dec_batch]), s_re, s_im, ffn_hist(cg_s[:dec_batch], cv_s[:dec_batch]))
```

```python
import functools

import jax
import jax.numpy as jnp
from jax import lax
from jax.experimental import pallas as pl
from jax.experimental.pallas import tpu as pltpu

D_MODEL = 2048
W_A = 1024
W_B = 1024
SSM_H = 16
SSM_G = 64
SSM_P = 64
D_FF = 5632
CHUNK = 64
N_META = 16
EPS = 1e-6

BF16 = jnp.bfloat16
F32 = jnp.float32
HIGHEST = lax.Precision.HIGHEST

CARRY_ROWS = 8
ROW_TILE = 512
FFN_ROW_TILE = 512
FF_TILE = 512
MXU_COLS = 256
GATE_ROWS = 128
SLAB = 128
SLAB_GROUPS = SLAB // SSM_H
SSM_ROWS = 272
VMEM_LIMIT = 56 * 1024 * 1024


def _dot(a, b):
    return jnp.dot(a, b, preferred_element_type=F32)


def _dot_hi(a, b):
    return jnp.dot(a, b, preferred_element_type=F32, precision=HIGHEST)


def _rmsnorm(x, g):
    ms = jnp.mean(x * x, axis=-1, keepdims=True)
    return (x * lax.rsqrt(ms + EPS)) * g


def _params(n_axes):
    return pltpu.CompilerParams(
        dimension_semantics=("arbitrary",) * n_axes, vmem_limit_bytes=VMEM_LIMIT)


def _resident(shape):
    return pl.BlockSpec(shape, lambda *_: (0,) * len(shape), pipeline_mode=pl.Buffered(1))


def _stash_rows(ext_ref, v, segs, cols=slice(None)):
    for k, (s, n) in enumerate(segs):
        base = s + CARRY_ROWS * (k + 1)
        ext_ref[base:base + n, cols] = v[s:s + n]


def _conv_prepare(ext_ref, init_ref, cout_ref, carry_ref, segs, chain_tiles):
    for k, (s, n) in enumerate(segs):
        base = s + CARRY_ROWS * k
        ext_ref[base:base + CARRY_ROWS, :] = carry_ref[...] if chain_tiles else init_ref[k]
        last = ext_ref[base + n:base + n + CARRY_ROWS, :]
        cout_ref[k] = last
        if chain_tiles:
            carry_ref[...] = last


def _conv_piece(ext_ref, w_ref, base, rows, cols=slice(None)):
    r0, r1 = rows
    t2 = ext_ref[base + 6 + r0:base + 6 + r1, cols]
    t1 = ext_ref[base + 7 + r0:base + 7 + r1, cols]
    t0 = ext_ref[base + 8 + r0:base + 8 + r1, cols]
    return t2 * w_ref[0:1, cols] + t1 * w_ref[1:2, cols] + t0 * w_ref[2:3, cols]


def _start_stream(init_ref, carry_ref, tile, chain_tiles):
    if chain_tiles:
        @pl.when(tile % chain_tiles == 0)
        def _():
            carry_ref[...] = init_ref[0]


def _mixer_a_kernel(x_ref, g_ref, w_ref, cw_ref, init_ref, hn_ref, outa_ref, cout_ref,
                    ext_ref, carry_ref, *, segs, chain_tiles):
    _start_stream(init_ref, carry_ref, pl.program_id(0), chain_tiles)
    hn = _rmsnorm(x_ref[...], g_ref[...]).astype(BF16)
    hn_ref[...] = hn
    zb = _dot(hn, w_ref[:, 0:W_A])
    zc = _dot(hn, w_ref[:, W_A:2 * W_A])
    zh = _dot(hn, w_ref[:, 2 * W_A:3 * W_A])
    _stash_rows(ext_ref, zc * zh, segs)
    _conv_prepare(ext_ref, init_ref, cout_ref, carry_ref, segs, chain_tiles)
    for k, (s, n) in enumerate(segs):
        conv = _conv_piece(ext_ref, cw_ref, s + CARRY_ROWS * k, (0, n))
        outa_ref[s:s + n, :] = (zb[s:s + n] * conv).astype(BF16)


def _mixer_a(x, g, w_a, conv_w, init, *, tm, segs, chain_tiles):
    rows = x.shape[0]
    n_tiles = rows // tm
    nseg = len(segs)
    n_out_seq = n_tiles // chain_tiles if chain_tiles else 1
    seq = (lambda i: (i // chain_tiles, 0, 0)) if chain_tiles else (lambda i: (0, 0, 0))
    return pl.pallas_call(
        functools.partial(_mixer_a_kernel, segs=segs, chain_tiles=chain_tiles),
        grid=(n_tiles,),
        in_specs=[
            pl.BlockSpec((tm, D_MODEL), lambda i: (i, 0)),
            _resident((1, D_MODEL)),
            _resident((D_MODEL, 3 * W_A)),
            _resident((3, W_A)),
            pl.BlockSpec((nseg, CARRY_ROWS, W_A), lambda i: (0, 0, 0)),
        ],
        out_specs=[
            pl.BlockSpec((tm, D_MODEL), lambda i: (i, 0)),
            pl.BlockSpec((tm, W_A), lambda i: (i, 0)),
            pl.BlockSpec((nseg, CARRY_ROWS, W_A), seq),
        ],
        out_shape=[
            jax.ShapeDtypeStruct((rows, D_MODEL), BF16),
            jax.ShapeDtypeStruct((rows, W_A), BF16),
            jax.ShapeDtypeStruct((n_out_seq * nseg, CARRY_ROWS, W_A), F32),
        ],
        scratch_shapes=[
            pltpu.VMEM((tm + CARRY_ROWS * nseg, W_A), F32),
            pltpu.VMEM((CARRY_ROWS, W_A), F32),
        ],
        compiler_params=_params(1),
        name="mixer_a",
    )(x, g, w_a, conv_w, init)


def _transpose_blocks(x):
    rows = x.shape[0]
    shape3 = (rows // 8, 8, SLAB)
    blk = lax.broadcasted_iota(jnp.int32, shape3, 2) >> 4
    x = pltpu.roll(x, 0, 1, stride=SSM_H, stride_axis=0)
    x3 = x.reshape(shape3)
    for d in (4, 2, 1):
        x3 = jnp.where((blk & d) != 0, pltpu.roll(x3, 8 - d, 1), x3)
    x3 = pltpu.roll(x3.reshape(rows, SLAB), 0, 1, stride=SSM_H, stride_axis=0).reshape(shape3)
    r = lax.broadcasted_iota(jnp.int32, shape3, 1) & 3
    x3 = jnp.where(r == 0, x3, jnp.where(r == 1, pltpu.roll(x3, 2, 1),
                                        jnp.where(r == 2, pltpu.roll(x3, 4, 1), pltpu.roll(x3, 6, 1))))
    return x3.reshape(rows, SLAB)


def _swap_outer_sublane(a):
    p = lax.broadcasted_iota(jnp.int32, a.shape, 0)
    q = lax.broadcasted_iota(jnp.int32, a.shape, 2)
    for d in (4, 2, 1):
        p_hi, q_hi = (p & d) != 0, (q & d) != 0
        up = pltpu.roll(jnp.roll(a, d, axis=0), 8 - d, 2)
        dn = pltpu.roll(jnp.roll(a, -d, axis=0), d, 2)
        a = jnp.where(jnp.logical_and(p_hi, jnp.logical_not(q_hi)), up,
                      jnp.where(jnp.logical_and(jnp.logical_not(p_hi), q_hi), dn, a))
    return a


OCTETS = CHUNK // 8
TILE_CHUNKS = 8


def _to_grouped(x, dst_ref, lanes):
    xs = _transpose_blocks(x)
    n_c = x.shape[0] // CHUNK
    for c0 in range(0, n_c - n_c % TILE_CHUNKS, TILE_CHUNKS):
        tile = xs[c0 * CHUNK:(c0 + TILE_CHUNKS) * CHUNK].reshape(TILE_CHUNKS, OCTETS, 8, SLAB)
        dst_ref[:, :, c0:c0 + TILE_CHUNKS, lanes] = _swap_outer_sublane(tile)
    for c in range(n_c - n_c % TILE_CHUNKS, n_c):
        for i in range(OCTETS):
            for s in range(8):
                r = c * CHUNK + 8 * i + s
                dst_ref[s, i, c:c + 1, lanes] = xs[r:r + 1]


def _from_grouped(src_ref, lanes, n_c, tail_ref):
    parts = []
    for c0 in range(0, n_c - n_c % TILE_CHUNKS, TILE_CHUNKS):
        tile = _swap_outer_sublane(src_ref[:, :, c0:c0 + TILE_CHUNKS, lanes])
        parts.append(tile.reshape(TILE_CHUNKS * CHUNK, SLAB))
    for k, c in enumerate(range(n_c - n_c % TILE_CHUNKS, n_c)):
        for i in range(OCTETS):
            for s in range(8):
                r = k * CHUNK + 8 * i + s
                tail_ref[r:r + 1, lanes] = src_ref[s, i, c:c + 1, lanes]
    if n_c % TILE_CHUNKS:
        parts.append(tail_ref[0:(n_c % TILE_CHUNKS) * CHUNK, lanes])
    xs = parts[0] if len(parts) == 1 else jnp.concatenate(parts, axis=0)
    return _transpose_blocks(xs)


GATE_COLS = 2 * D_MODEL


def _gates_u_kernel(hn_ref, w_ref, gates_ref, u_ref, *, n_chunk_rows):
    hn = hn_ref[...]
    u = _dot(hn, w_ref[:, GATE_COLS:])
    n_c = hn.shape[0] // CHUNK
    if n_chunk_rows > n_c:
        u_ref[:, :, n_c:, :] = jnp.zeros((8, OCTETS, n_chunk_rows - n_c, W_B), F32)
    n_gate = GATE_COLS // 1024
    slabs_per_gate = W_B // SLAB // n_gate
    for k in range(n_gate):
        cols = slice(k * 1024, (k + 1) * 1024)
        gates_ref[:, cols] = jax.nn.sigmoid(_dot(hn, w_ref[:, cols])).astype(BF16)
        for j in range(k * slabs_per_gate, (k + 1) * slabs_per_gate):
            lanes = slice(j * SLAB, (j + 1) * SLAB)
            _to_grouped(u[:, lanes], u_ref, lanes)


def _gates_u(hn, w_ug, *, tm):
    rows = hn.shape[0]
    n_tiles = rows // tm
    n_c = tm // CHUNK
    n_chunk_rows = n_c if n_tiles > 1 else -(-n_c // 16) * 16
    return pl.pallas_call(
        functools.partial(_gates_u_kernel, n_chunk_rows=n_chunk_rows),
        grid=(n_tiles,),
        in_specs=[
            pl.BlockSpec((tm, D_MODEL), lambda i: (i, 0)),
            _resident((D_MODEL, GATE_COLS + W_B)),
        ],
        out_specs=[pl.BlockSpec((tm, GATE_COLS), lambda i: (i, 0)),
                   pl.BlockSpec((8, OCTETS, n_chunk_rows, W_B), lambda i: (0, 0, i, 0))],
        out_shape=[jax.ShapeDtypeStruct((rows, GATE_COLS), BF16),
                   jax.ShapeDtypeStruct((8, OCTETS, n_tiles * n_chunk_rows, W_B), F32)],
        compiler_params=_params(1),
        name="gates_u",
    )(hn, w_ug)


def _ssm_group(u, lre_c, lim_c, ldt_c, lre_r, lim_r, ldt_r, bta, btb, ctr, cti, ddiag, h0,
               sel_lag, sel_ch, sel_row, t_ref, xs_ref):
    lane128 = lax.broadcasted_iota(jnp.int32, (1, 128), 1)
    sgn_conj = jnp.where(lane128 < SSM_P, 1.0, -1.0).astype(F32)
    sgn_mul = -sgn_conj

    def swap(h):
        return pltpu.roll(h, SSM_P, axis=1)

    def cmul(ar, ai, h):
        return ar * h + (ai * sgn_mul) * swap(h)

    dt_c = jnp.exp(ldt_c)
    k_row = lax.broadcasted_iota(jnp.int32, (1, 128), 1).astype(F32)
    mag = jnp.exp((lre_c * dt_c) * k_row)
    ang = (lim_c * dt_c) * k_row
    apt_re = mag * jnp.cos(ang)
    apt_im = mag * jnp.sin(ang)
    ae_re = _dot_hi(apt_re, sel_lag)
    ae_im = _dot_hi(apt_im, sel_lag)
    cte_re = _dot_hi(ctr, sel_ch)
    cte_im = _dot_hi(cti, sel_ch)
    ca_re = cte_re * ae_re - cte_im * ae_im
    ca_im = cte_re * ae_im + cte_im * ae_re
    a1_re = apt_re[:, 1:2]
    a1_im = apt_im[:, 1:2]
    ca1_re = ca_re * a1_re - ca_im * a1_im
    ca1_im = ca_re * a1_im + ca_im * a1_re

    dt_r = jnp.exp(ldt_r)
    xr = lre_r * dt_r
    xi = lim_r * dt_r
    a_re = jnp.exp(xr) * jnp.cos(xi)
    a_im = jnp.exp(xr) * jnp.sin(xi)
    den = lre_r * lre_r + lim_r * lim_r
    nr = a_re - 1.0
    cr = (nr * lre_r + a_im * lim_r) / den
    ci = (a_im * lre_r - nr * lim_r) / den
    bb_a = bta * cr + btb * ci
    bb_b = btb * cr - bta * ci

    k2 = _dot_hi(bb_a * sgn_conj, jnp.concatenate([ca_re, ca_im], axis=0))
    lane1k = lax.broadcasted_iota(jnp.int32, (SSM_H, CHUNK * SSM_H), 1)
    k2 = k2 + jnp.where(lane1k < SSM_H, _dot_hi(ddiag, sel_ch), 0.0)
    k2z = jnp.concatenate([k2, jnp.zeros_like(k2)], axis=1)
    for r in range(8):
        kr = pltpu.roll(k2z, SSM_H * r, axis=1) if r else k2z
        for q in range(CHUNK // 8):
            m = 8 * q + r
            rows = slice(SSM_H * m, SSM_H * (m + 1))
            if q:
                t_ref[rows, 0:128 * q] = jnp.zeros((SSM_H, 128 * q), BF16)
            t_ref[rows, 128 * q:] = kr[:, 0:CHUNK * SSM_H - 128 * q].astype(BF16)

    e_col = (CHUNK - 1 - lax.broadcasted_iota(jnp.int32, (CHUNK, 1), 0)).astype(F32)
    apm = jnp.exp(e_col * xr)
    apa = e_col * xi
    apx_re = _dot_hi(sel_row, apm * jnp.cos(apa))
    apx_im = _dot_hi(sel_row, apm * jnp.sin(apa))
    tile_rows = lambda b: jnp.broadcast_to(b[None], (CHUNK, SSM_H, 128)).reshape(CHUNK * SSM_H, 128)
    bcat = (apx_re * tile_rows(bb_a) + apx_im * tile_rows(bb_b)).astype(BF16)
    ccat = jnp.concatenate([ca1_re, -ca1_im], axis=0).astype(BF16)

    s = _dot(u, bcat)
    al_re = jnp.exp(CHUNK * xr) * jnp.cos(CHUNK * xi)
    al_im = jnp.exp(CHUNK * xr) * jnp.sin(CHUNK * xi)

    h_meta = s[264:265]
    h_sample = cmul(al_re, al_im, h0) + s[256:264]

    n_p, n_c = 256, 128
    chunk = lax.broadcasted_iota(jnp.int32, (n_p, 128), 0) & (n_c - 1)
    x = s[0:n_p] + jnp.where(chunk == 0, cmul(al_re, al_im, h_meta), 0.0)
    head = n_c
    xs_ref[0:head, :] = jnp.zeros((head, 128), F32)
    ar, ai = al_re, al_im
    for step in range(7):
        sh = 1 << step
        xs_ref[head:head + n_p, :] = x
        x = x + jnp.where(chunk >= sh, cmul(ar, ai, xs_ref[head - sh:head + n_p - sh, :]), 0.0)
        ar, ai = ar * ar - ai * ai, 2.0 * ar * ai
    xs_ref[head:head + n_p, :] = x
    h_in_prompt = jnp.where(chunk == 0, h_meta, xs_ref[head - 1:head + n_p - 1, :])
    h_in = jnp.concatenate([h_in_prompt, h0, jnp.zeros((SSM_ROWS - n_p - 8, 128), F32)], axis=0)

    h_in16 = h_in.astype(BF16)
    y = jnp.concatenate(
        [_dot(u[:, 0:hi], t_ref[0:hi, hi - MXU_COLS:hi]) + _dot(h_in16, ccat[:, hi - MXU_COLS:hi])
         for hi in range(MXU_COLS, CHUNK * SSM_H + 1, MXU_COLS)], axis=1)
    h_fin = jnp.concatenate([x[n_c - 1:n_c], x[n_p - 1:n_p], jnp.zeros((6, 128), F32), h_sample], axis=0)
    return y, h_fin


def _ssm_kernel(up_ref, us_ref, lre_c_ref, lim_c_ref, ldt_c_ref, lre_r_ref, lim_r_ref, ldt_r_ref,
                bta_ref, btb_ref, ctr_ref, cti_ref, dd_ref, h0_ref, yp_ref, ys_ref, hfin_ref,
                t_ref, xs_ref):
    kl = CHUNK * SSM_H
    lag_of_lane = lax.broadcasted_iota(jnp.int32, (128, kl), 1) >> 4
    sel_lag = (lax.broadcasted_iota(jnp.int32, (128, kl), 0) == lag_of_lane).astype(F32)
    ch_of_lane = lax.broadcasted_iota(jnp.int32, (SSM_H, kl), 1) & (SSM_H - 1)
    sel_ch = (lax.broadcasted_iota(jnp.int32, (SSM_H, kl), 0) == ch_of_lane).astype(F32)
    lag_of_row = lax.broadcasted_iota(jnp.int32, (kl, CHUNK), 0) >> 4
    sel_row = (lax.broadcasted_iota(jnp.int32, (kl, CHUNK), 1) == lag_of_row).astype(F32)
    n_p = up_ref.shape[2]
    for s in range(SLAB_GROUPS):
        u_p = jnp.concatenate([up_ref[s, i] for i in range(OCTETS)], axis=1)
        u_s = jnp.concatenate([us_ref[s, i] for i in range(OCTETS)], axis=1)
        u = jnp.concatenate([u_p, u_s], axis=0).astype(BF16)
        y, h_fin = _ssm_group(
            u, lre_c_ref[s], lim_c_ref[s], ldt_c_ref[s], lre_r_ref[s], lim_r_ref[s],
            ldt_r_ref[s], bta_ref[s], btb_ref[s], ctr_ref[s], cti_ref[s], dd_ref[s], h0_ref[s],
            sel_lag, sel_ch, sel_row, t_ref.at[s % 2], xs_ref)
        hfin_ref[s] = h_fin
        for i in range(OCTETS):
            lanes = slice(SLAB * i, SLAB * (i + 1))
            yp_ref[s, i] = y[0:n_p, lanes]
            ys_ref[s, i] = y[n_p:, lanes]


def _ssm(u_p, u_s, lre_c, lim_c, ldt_c, lre_r, lim_r, ldt_r, bta, btb, ctr, cti, ddiag, h0):
    gb = SLAB_GROUPS
    per_group = lambda *tail: pl.BlockSpec((gb,) + tail, lambda j: (j,) + (0,) * len(tail))
    slab = lambda a: pl.BlockSpec(a.shape[:3] + (SLAB,), lambda j: (0, 0, 0, j))
    assert u_p.shape[2] + u_s.shape[2] == SSM_ROWS
    return pl.pallas_call(
        _ssm_kernel,
        grid=(SSM_G // gb,),
        in_specs=[
            slab(u_p), slab(u_s),
            per_group(SSM_P, 1), per_group(SSM_P, 1), per_group(SSM_P, 1),
            per_group(1, 128), per_group(1, 128), per_group(1, 128),
            per_group(SSM_H, 128), per_group(SSM_H, 128),
            per_group(SSM_P, SSM_H), per_group(SSM_P, SSM_H),
            per_group(SSM_H, SSM_H),
            per_group(8, 128),
        ],
        out_specs=[slab(u_p), slab(u_s), per_group(16, 128)],
        out_shape=[
            jax.ShapeDtypeStruct(u_p.shape, F32),
            jax.ShapeDtypeStruct(u_s.shape, F32),
            jax.ShapeDtypeStruct((SSM_G, 16, 128), F32),
        ],
        scratch_shapes=[
            pltpu.VMEM((2, CHUNK * SSM_H, CHUNK * SSM_H), BF16),
            pltpu.VMEM((128 + 256, 128), F32),
        ],
        compiler_params=_params(1),
        name="ssm",
    )(u_p, u_s, lre_c, lim_c, ldt_c, lre_r, lim_r, ldt_r, bta, btb, ctr, cti, ddiag, h0)


def _merge_kernel(y_ref, outa_ref, sga_ref, sgb_ref, x_ref, gluw_ref, glub_ref, pa_ref,
                  pb_ref, wo_ref, x1_ref, yt_ref, merged_ref, tail_ref):
    tm = x_ref.shape[0]
    n_c = tm // CHUNK
    n_piece = 4
    width = D_MODEL // n_piece
    slabs_per_piece = W_B // SLAB // n_piece
    outa = outa_ref[...]
    proj = []
    for k in range(n_piece):
        proj.append(_dot(outa, pa_ref[:, k * width:(k + 1) * width]))
        for j in range(k * slabs_per_piece, (k + 1) * slabs_per_piece):
            lanes = slice(j * SLAB, (j + 1) * SLAB)
            yt_ref[:, lanes] = jax.nn.gelu(_from_grouped(y_ref, lanes, n_c, tail_ref))
    yb = yt_ref[...]
    glu = _dot(yb.astype(BF16), gluw_ref[...]) + glub_ref[...]
    out_b = (yb * jax.nn.sigmoid(glu)).astype(BF16)
    for k in range(n_piece):
        cols = slice(k * width, (k + 1) * width)
        merged_ref[:, cols] = (sga_ref[:, cols].astype(F32) * proj[k]
                               + sgb_ref[:, cols].astype(F32) * _dot(out_b, pb_ref[:, cols])).astype(BF16)
    x1_ref[...] = x_ref[...] + _dot(merged_ref[...], wo_ref[...])


def _merge(y_grouped, outa, ug, x, glu_w, glu_b, proj_a, proj_b, w_out, *, tm):
    rows = x.shape[0]
    n_tiles = rows // tm
    n_chunk_rows = y_grouped.shape[2] // n_tiles
    return pl.pallas_call(
        _merge_kernel,
        grid=(n_tiles,),
        in_specs=[
            pl.BlockSpec((8, OCTETS, n_chunk_rows, W_B), lambda i: (0, 0, i, 0)),
            pl.BlockSpec((tm, W_A), lambda i: (i, 0)),
            pl.BlockSpec((tm, D_MODEL), lambda i: (i, 0)),
            pl.BlockSpec((tm, D_MODEL), lambda i: (i, 1)),
            pl.BlockSpec((tm, D_MODEL), lambda i: (i, 0)),
            _resident((W_B, W_B)),
            _resident((1, W_B)),
            _resident((W_A, D_MODEL)),
            _resident((W_B, D_MODEL)),
            _resident((D_MODEL, D_MODEL)),
        ],
        out_specs=pl.BlockSpec((tm, D_MODEL), lambda i: (i, 0)),
        out_shape=jax.ShapeDtypeStruct((rows, D_MODEL), F32),
        scratch_shapes=[
            pltpu.VMEM((tm, W_B), F32),
            pltpu.VMEM((tm, D_MODEL), BF16),
            pltpu.VMEM((CHUNK * max(tm // CHUNK % TILE_CHUNKS, 1), W_B), F32),
        ],
        compiler_params=_params(1),
        name="merge",
    )(y_grouped, outa, ug, ug, x, glu_w, glu_b, proj_a, proj_b, w_out)


def _ffn_kernel(x1_ref, g_ref, wug_ref, wuv_ref, cw_all_ref, cb_all_ref, wd_ref,
                gfin_ref, initg_all_ref, initv_all_ref, y_ref, coutg_ref, coutv_ref,
                hn_ref, extg0_ref, extg1_ref, extv0_ref, extv1_ref, act0_ref, act1_ref,
                carryg_ref, carryv_ref, *, segs, chain_tiles, n_f, n_items):
    t = pl.program_id(0)
    item1 = jnp.clip(t - 1, 0, n_items - 1)
    item2 = jnp.clip(t - 2, 0, n_items - 1)
    f1, tile1 = item1 % n_f, item1 // n_f
    f2 = item2 % n_f
    draining = t >= 2
    cwg_ref, cwv_ref = cw_all_ref.at[f1], cw_all_ref.at[n_f + f1]
    bg_ref, bv_ref = cb_all_ref.at[f1], cb_all_ref.at[n_f + f1]
    initg_ref, initv_ref = initg_all_ref.at[f1], initv_all_ref.at[f1]

    @pl.when(t == 0)
    def _():
        extg1_ref[...] = jnp.zeros(extg1_ref.shape, F32)
        extv1_ref[...] = jnp.zeros(extv1_ref.shape, F32)
        act0_ref[...] = jnp.zeros(act0_ref.shape, BF16)
        y_ref[...] = jnp.zeros(y_ref.shape, F32)

    @pl.when(jnp.minimum(t, n_items - 1) % n_f == 0)
    def _():
        hn_ref[...] = _rmsnorm(x1_ref[...], g_ref[...]).astype(BF16)

    @pl.when(jnp.logical_and(f2 == 0, draining))
    def _():
        y_ref[...] = x1_ref[...]

    _start_stream(initg_ref, carryg_ref.at[f1], tile1, chain_tiles)
    _start_stream(initv_ref, carryv_ref.at[f1], tile1, chain_tiles)

    bufs = ((extg0_ref, extv0_ref, act0_ref), (extg1_ref, extv1_ref, act1_ref))
    for par in (0, 1):
        @pl.when(t % 2 == par)
        def _(par=par):
            up_g, up_v, act_in = bufs[par]
            conv_src_g, conv_src_v, act_out = bufs[1 - par]

            def up_piece(dst_ref, w_ref, cols):
                return lambda: _stash_rows(dst_ref, _dot(hn_ref[...], w_ref[:, cols]), segs, cols)

            def down_piece(cols):
                def run():
                    y_ref[:, cols] += _dot(act_in[...], wd_ref[:, cols])
                return run

            def gate_piece(k, s, rows, cols):
                def run():
                    base = s + CARRY_ROWS * k
                    g = _conv_piece(conv_src_g, cwg_ref, base, rows, cols) + bg_ref[:, cols]
                    v = _conv_piece(conv_src_v, cwv_ref, base, rows, cols) + bv_ref[:, cols]
                    act_out[s + rows[0]:s + rows[1], cols] = (jax.nn.silu(g) * v).astype(BF16)
                return run

            col_slices = lambda width, step: [slice(c, c + step) for c in range(0, width, step)]
            mxu = ([(up_piece(up_g, wug_ref, c), 4) for c in col_slices(FF_TILE, MXU_COLS)]
                   + [(up_piece(up_v, wuv_ref, c), 4) for c in col_slices(FF_TILE, MXU_COLS)]
                   + [(down_piece(c), 1) for c in col_slices(D_MODEL, MXU_COLS)])
            vpu = [gate_piece(k, s, (r, min(r + GATE_ROWS, n)), c)
                   for c in col_slices(FF_TILE, 128)
                   for k, (s, n) in enumerate(segs) for r in range(0, n, GATE_ROWS)]

            _conv_prepare(conv_src_g, initg_ref, coutg_ref.at[:, f1], carryg_ref.at[f1], segs, chain_tiles)
            _conv_prepare(conv_src_v, initv_ref, coutv_ref.at[:, f1], carryv_ref.at[f1], segs, chain_tiles)
            total = sum(weight for _, weight in mxu)
            done, issued = 0, 0
            for run, weight in mxu:
                run()
                done += weight
                while issued < len(vpu) and issued * total < done * len(vpu):
                    vpu[issued]()
                    issued += 1

    @pl.when(jnp.logical_and(f2 == n_f - 1, draining))
    def _():
        y_ref[...] = _rmsnorm(y_ref[...], gfin_ref[...])


def _ffn(x1, g, w_up, conv_w, conv_b, w_down, g_fin, init_g, init_v, *, tm, segs, chain_tiles):
    rows = x1.shape[0]
    n_tiles = rows // tm
    n_f = D_FF // FF_TILE
    n_items = n_tiles * n_f
    nseg = len(segs)
    n_out_seq = n_tiles // chain_tiles if chain_tiles else 1
    item = lambda t, lag: jnp.clip(t - lag, 0, n_items - 1)
    ff = lambda lag: (lambda t: item(t, lag) % n_f)
    tile = lambda lag: (lambda t: item(t, lag) // n_f)
    seq1 = lambda t: tile(1)(t) // chain_tiles if chain_tiles else 0
    cw_all = conv_w.reshape(3, 2 * n_f, FF_TILE).transpose(1, 0, 2)
    cb_all = conv_b.reshape(2 * n_f, 1, FF_TILE)
    per_tile = lambda a: a.reshape(a.shape[0], CARRY_ROWS, n_f, FF_TILE).transpose(2, 0, 1, 3)
    init_g, init_v = per_tile(init_g), per_tile(init_v)
    cout_block = (nseg, n_f, CARRY_ROWS, FF_TILE)
    cout_shape = jax.ShapeDtypeStruct((n_out_seq * nseg, n_f, CARRY_ROWS, FF_TILE), F32)
    ext_shape = (tm + CARRY_ROWS * nseg, FF_TILE)
    y, cout_g, cout_v = pl.pallas_call(
        functools.partial(_ffn_kernel, segs=segs, chain_tiles=chain_tiles, n_f=n_f, n_items=n_items),
        grid=(n_items + 2,),
        in_specs=[
            pl.BlockSpec((tm, D_MODEL), lambda t: (tile(0)(t), 0)),
            pl.BlockSpec((1, D_MODEL), lambda t: (0, 0)),
            pl.BlockSpec((D_MODEL, FF_TILE), lambda t: (0, ff(0)(t))),
            pl.BlockSpec((D_MODEL, FF_TILE), lambda t: (0, n_f + ff(0)(t))),
            _resident(cw_all.shape),
            _resident(cb_all.shape),
            pl.BlockSpec((FF_TILE, D_MODEL), lambda t: (ff(2)(t), 0)),
            pl.BlockSpec((1, D_MODEL), lambda t: (0, 0)),
            _resident(init_g.shape),
            _resident(init_v.shape),
        ],
        out_specs=[
            pl.BlockSpec((tm, D_MODEL), lambda t: (tile(2)(t), 0)),
            pl.BlockSpec(cout_block, lambda t: (seq1(t), 0, 0, 0)),
            pl.BlockSpec(cout_block, lambda t: (seq1(t), 0, 0, 0)),
        ],
        out_shape=[jax.ShapeDtypeStruct((rows, D_MODEL), F32), cout_shape, cout_shape],
        scratch_shapes=[
            pltpu.VMEM((tm, D_MODEL), BF16),
            pltpu.VMEM(ext_shape, F32), pltpu.VMEM(ext_shape, F32),
            pltpu.VMEM(ext_shape, F32), pltpu.VMEM(ext_shape, F32),
            pltpu.VMEM((tm, FF_TILE), BF16), pltpu.VMEM((tm, FF_TILE), BF16),
            pltpu.VMEM((n_f, CARRY_ROWS, FF_TILE), F32),
            pltpu.VMEM((n_f, CARRY_ROWS, FF_TILE), F32),
        ],
        compiler_params=_params(1),
        name="ffn",
    )(x1, g, w_up, w_up, cw_all, cb_all, w_down, g_fin, init_g, init_v)
    widen = lambda c: c.transpose(0, 2, 1, 3).reshape(c.shape[0], CARRY_ROWS, D_FF)
    return y, widen(cout_g), widen(cout_v)


def _history_block(buf):
    return jnp.pad(buf, ((0, 0), (CARRY_ROWS - buf.shape[1], 0), (0, 0)))


def kernel(x_prompt, x_sample, cache_conv_a, state_ssm_re, state_ssm_im, cache_ffn_conv,
           meta_tokens, norm_mix_g, w_in, conv_a_w, ssm_lambda_re, ssm_lambda_im, ssm_log_dt,
           ssm_b_re, ssm_b_im, ssm_c_re, ssm_c_im, ssm_d, glu_w, glu_b, proj_a, proj_b,
           w_out, norm_ffn_g, w_up, ffn_conv_w, ffn_conv_b, w_down, norm_final_g):
    batch, seq, _ = x_prompt.shape
    dec_batch, dec_seq, _ = x_sample.shape
    assert dec_seq == CHUNK and seq % CHUNK == 0 and seq % ROW_TILE == 0 and seq % FFN_ROW_TILE == 0
    n_chunks = seq // CHUNK
    assert batch * n_chunks == 256 and dec_batch == 8

    w_a = w_in[0, :, :3 * W_A].astype(BF16)
    w_ug = jnp.concatenate([w_in[0, :, 3 * W_A + W_B:], w_in[0, :, 3 * W_A:3 * W_A + W_B]],
                           axis=1).astype(BF16)
    glu_w16, proj_a16, proj_b16 = glu_w[0].astype(BF16), proj_a[0].astype(BF16), proj_b[0].astype(BF16)
    w_out16, w_up16, w_down16 = w_out[0].astype(BF16), w_up[0].astype(BF16), w_down[0].astype(BF16)
    g_mix, g_ffn, g_fin = norm_mix_g[0][None], norm_ffn_g[0][None], norm_final_g[None]
    glu_b2, ffn_b2 = glu_b[0][None], ffn_conv_b[0][None]

    n_s = dec_batch * dec_seq
    n_pad = CHUNK - N_META
    rows_s = n_s + CHUNK
    x_s = jnp.concatenate([x_sample.reshape(n_s, D_MODEL), jnp.zeros((n_pad, D_MODEL), F32), meta_tokens],
                          axis=0)
    segs_s = (tuple((b * dec_seq, dec_seq) for b in range(dec_batch))
              + ((n_s, n_pad), (n_s + n_pad, N_META)))
    meta_seg = dec_batch + 1
    x_p = x_prompt.reshape(batch * seq, D_MODEL)
    segs_p = ((0, ROW_TILE),)
    chain = seq // ROW_TILE

    zero_hist = lambda c: jnp.zeros((1, CARRY_ROWS, c), F32)
    init_a_s = jnp.concatenate([_history_block(cache_conv_a[0]), zero_hist(W_A), zero_hist(W_A)], axis=0)
    hn_s, outa_s, ca_s = _mixer_a(x_s, g_mix, w_a, conv_a_w[0], init_a_s,
                                  tm=rows_s, segs=segs_s, chain_tiles=0)
    hn_p, outa_p, ca_p = _mixer_a(x_p, g_mix, w_a, conv_a_w[0], ca_s[meta_seg:],
                                  tm=ROW_TILE, segs=segs_p, chain_tiles=chain)
    gates_s, u_s = _gates_u(hn_s, w_ug, tm=rows_s)
    gates_p, u_p = _gates_u(hn_p, w_ug, tm=ROW_TILE)

    dup = lambda v: jnp.concatenate([v, v], axis=-1)[:, None, :]
    lre, lim = ssm_lambda_re[0], ssm_lambda_im[0]
    ldt = jnp.broadcast_to(ssm_log_dt[0][:, None], (SSM_G, SSM_P))
    bt_re, bt_im = ssm_b_re[0].swapaxes(1, 2), ssm_b_im[0].swapaxes(1, 2)
    bta = jnp.concatenate([bt_re, bt_im], axis=-1)
    btb = jnp.concatenate([-bt_im, bt_re], axis=-1)
    ddiag = ssm_d[0][:, :, None] * jnp.eye(SSM_H, dtype=F32)
    h0 = jnp.concatenate([state_ssm_re[0], state_ssm_im[0]], axis=-1).swapaxes(0, 1)
    yb_p, yb_s, h_fin = _ssm(u_p, u_s, lre[:, :, None], lim[:, :, None], ldt[:, :, None],
                             dup(lre), dup(lim), dup(ldt), bta, btb,
                             ssm_c_re[0].swapaxes(1, 2), ssm_c_im[0].swapaxes(1, 2), ddiag, h0)

    x1_s = _merge(yb_s, outa_s, gates_s, x_s, glu_w16, glu_b2, proj_a16, proj_b16, w_out16, tm=rows_s)
    x1_p = _merge(yb_p, outa_p, gates_p, x_p, glu_w16, glu_b2, proj_a16, proj_b16, w_out16, tm=ROW_TILE)

    hist_f = _history_block(cache_ffn_conv[0])
    init_g_s = jnp.concatenate([hist_f[:, :, :D_FF], zero_hist(D_FF), zero_hist(D_FF)], axis=0)
    init_v_s = jnp.concatenate([hist_f[:, :, D_FF:], zero_hist(D_FF), zero_hist(D_FF)], axis=0)
    y_s, cg_s, cv_s = _ffn(x1_s, g_ffn, w_up16, ffn_conv_w[0], ffn_b2, w_down16, g_fin,
                           init_g_s, init_v_s, tm=rows_s, segs=segs_s, chain_tiles=0)
    y_p, cg_p, cv_p = _ffn(x1_p, g_ffn, w_up16, ffn_conv_w[0], ffn_b2, w_down16, g_fin,
                           cg_s[meta_seg:], cv_s[meta_seg:], tm=FFN_ROW_TILE,
                           segs=((0, FFN_ROW_TILE),), chain_tiles=seq // FFN_ROW_TILE)

    hist = lambda c: c[:, CARRY_ROWS - 2:][None]
    ffn_hist = lambda cg, cv: hist(jnp.concatenate([cg, cv], axis=-1))
    state = lambda h: (h[:, :, :SSM_P].swapaxes(0, 1)[None], h[:, :, SSM_P:].swapaxes(0, 1)[None])
    p_re, p_im = state(h_fin[:, 0:batch])
    s_re, s_im = state(h_fin[:, 8:8 + dec_batch])
    return (y_p.reshape(batch, seq, D_MODEL), y_s[:n_s].reshape(dec_batch, dec_seq, D_MODEL),
            hist(ca_p), p_re, p_im, ffn_hist(cg_p, cv_p),
            hist(ca_s[:dec_batch]), s_re, s_im, ffn_hist(cg_s[:dec_batch], cv_s[:dec_batch]))
```

```python
import functools

import jax
import jax.numpy as jnp
from jax import lax
from jax.experimental import pallas as pl
from jax.experimental.pallas import tpu as pltpu

D_MODEL = 2048
W_A = 1024
W_B = 1024
SSM_H = 16
SSM_G = 64
SSM_P = 64
D_FF = 5632
CHUNK = 64
N_META = 16
EPS = 1e-6

BF16 = jnp.bfloat16
F32 = jnp.float32
HIGHEST = lax.Precision.HIGHEST

CARRY_ROWS = 8
ROW_TILE = 512
FFN_ROW_TILE = 512
FF_TILE = 512
MXU_COLS = 256
GATE_ROWS = 128
SSM_INTERLEAVE = 4
T_COLS = 256
SLAB = 128
SLAB_GROUPS = SLAB // SSM_H
SSM_ROWS = 272
VMEM_LIMIT = 56 * 1024 * 1024


def _dot(a, b):
    return jnp.dot(a, b, preferred_element_type=F32)


def _dot_hi(a, b):
    return jnp.dot(a, b, preferred_element_type=F32, precision=HIGHEST)


def _split3(x):
    hi = x.astype(BF16)
    rest = x - hi.astype(F32)
    mid = rest.astype(BF16)
    return hi, mid, (rest - mid.astype(F32)).astype(BF16)


def _select_cols(x, sel):
    return sum(_dot(part, sel) for part in _split3(x))


def _select_rows(sel, x):
    return sum(_dot(sel, part) for part in _split3(x))


def _rmsnorm(x, g):
    ms = jnp.mean(x * x, axis=-1, keepdims=True)
    return (x * lax.rsqrt(ms + EPS)) * g


def _params(n_axes):
    return pltpu.CompilerParams(
        dimension_semantics=("arbitrary",) * n_axes, vmem_limit_bytes=VMEM_LIMIT)


def _resident(shape):
    return pl.BlockSpec(shape, lambda *_: (0,) * len(shape), pipeline_mode=pl.Buffered(1))


def _stash_rows(ext_ref, v, segs, cols=slice(None)):
    for k, (s, n) in enumerate(segs):
        base = s + CARRY_ROWS * (k + 1)
        ext_ref[base:base + n, cols] = v[s:s + n]


def _conv_prepare(ext_ref, init_ref, cout_ref, carry_ref, segs, chain_tiles):
    for k, (s, n) in enumerate(segs):
        base = s + CARRY_ROWS * k
        ext_ref[base:base + CARRY_ROWS, :] = carry_ref[...] if chain_tiles else init_ref[k]
        last = ext_ref[base + n:base + n + CARRY_ROWS, :]
        cout_ref[k] = last
        if chain_tiles:
            carry_ref[...] = last


def _conv_piece(ext_ref, w_ref, base, rows, cols=slice(None)):
    r0, r1 = rows
    t2 = ext_ref[base + 6 + r0:base + 6 + r1, cols]
    t1 = ext_ref[base + 7 + r0:base + 7 + r1, cols]
    t0 = ext_ref[base + 8 + r0:base + 8 + r1, cols]
    return t2 * w_ref[0:1, cols] + t1 * w_ref[1:2, cols] + t0 * w_ref[2:3, cols]


def _start_stream(init_ref, carry_ref, tile, chain_tiles):
    if chain_tiles:
        @pl.when(tile % chain_tiles == 0)
        def _():
            carry_ref[...] = init_ref[0]


def _mixer_a_kernel(x_ref, g_ref, w_ref, cw_ref, init_ref, hn_ref, outa_ref, cout_ref,
                    ext_ref, carry_ref, *, segs, chain_tiles):
    _start_stream(init_ref, carry_ref, pl.program_id(0), chain_tiles)
    hn = _rmsnorm(x_ref[...], g_ref[...]).astype(BF16)
    hn_ref[...] = hn
    zb = _dot(hn, w_ref[:, 0:W_A])
    zc = _dot(hn, w_ref[:, W_A:2 * W_A])
    zh = _dot(hn, w_ref[:, 2 * W_A:3 * W_A])
    _stash_rows(ext_ref, zc * zh, segs)
    _conv_prepare(ext_ref, init_ref, cout_ref, carry_ref, segs, chain_tiles)
    for k, (s, n) in enumerate(segs):
        conv = _conv_piece(ext_ref, cw_ref, s + CARRY_ROWS * k, (0, n))
        outa_ref[s:s + n, :] = (zb[s:s + n] * conv).astype(BF16)


def _mixer_a(x, g, w_a, conv_w, init, *, tm, segs, chain_tiles):
    rows = x.shape[0]
    n_tiles = rows // tm
    nseg = len(segs)
    n_out_seq = n_tiles // chain_tiles if chain_tiles else 1
    seq = (lambda i: (i // chain_tiles, 0, 0)) if chain_tiles else (lambda i: (0, 0, 0))
    return pl.pallas_call(
        functools.partial(_mixer_a_kernel, segs=segs, chain_tiles=chain_tiles),
        grid=(n_tiles,),
        in_specs=[
            pl.BlockSpec((tm, D_MODEL), lambda i: (i, 0)),
            _resident((1, D_MODEL)),
            _resident((D_MODEL, 3 * W_A)),
            _resident((3, W_A)),
            pl.BlockSpec((nseg, CARRY_ROWS, W_A), lambda i: (0, 0, 0)),
        ],
        out_specs=[
            pl.BlockSpec((tm, D_MODEL), lambda i: (i, 0)),
            pl.BlockSpec((tm, W_A), lambda i: (i, 0)),
            pl.BlockSpec((nseg, CARRY_ROWS, W_A), seq),
        ],
        out_shape=[
            jax.ShapeDtypeStruct((rows, D_MODEL), BF16),
            jax.ShapeDtypeStruct((rows, W_A), BF16),
            jax.ShapeDtypeStruct((n_out_seq * nseg, CARRY_ROWS, W_A), F32),
        ],
        scratch_shapes=[
            pltpu.VMEM((tm + CARRY_ROWS * nseg, W_A), F32),
            pltpu.VMEM((CARRY_ROWS, W_A), F32),
        ],
        compiler_params=_params(1),
        name="mixer_a",
    )(x, g, w_a, conv_w, init)


def _transpose_blocks(x):
    rows = x.shape[0]
    shape3 = (rows // 8, 8, SLAB)
    blk = lax.broadcasted_iota(jnp.int32, shape3, 2) >> 4
    x = pltpu.roll(x, 0, 1, stride=SSM_H, stride_axis=0)
    x3 = x.reshape(shape3)
    for d in (4, 2, 1):
        x3 = jnp.where((blk & d) != 0, pltpu.roll(x3, 8 - d, 1), x3)
    x3 = pltpu.roll(x3.reshape(rows, SLAB), 0, 1, stride=SSM_H, stride_axis=0).reshape(shape3)
    r = lax.broadcasted_iota(jnp.int32, shape3, 1) & 3
    x3 = jnp.where(r == 0, x3, jnp.where(r == 1, pltpu.roll(x3, 2, 1),
                                        jnp.where(r == 2, pltpu.roll(x3, 4, 1), pltpu.roll(x3, 6, 1))))
    return x3.reshape(rows, SLAB)


def _swap_outer_sublane(a):
    p = lax.broadcasted_iota(jnp.int32, a.shape, 0)
    q = lax.broadcasted_iota(jnp.int32, a.shape, 2)
    for d in (4, 2, 1):
        p_hi, q_hi = (p & d) != 0, (q & d) != 0
        up = pltpu.roll(jnp.roll(a, d, axis=0), 8 - d, 2)
        dn = pltpu.roll(jnp.roll(a, -d, axis=0), d, 2)
        a = jnp.where(jnp.logical_and(p_hi, jnp.logical_not(q_hi)), up,
                      jnp.where(jnp.logical_and(jnp.logical_not(p_hi), q_hi), dn, a))
    return a


OCTETS = CHUNK // 8
TILE_CHUNKS = 8


def _to_grouped(x, dst_ref, lanes):
    xs = _transpose_blocks(x)
    n_c = x.shape[0] // CHUNK
    for c0 in range(0, n_c - n_c % TILE_CHUNKS, TILE_CHUNKS):
        tile = xs[c0 * CHUNK:(c0 + TILE_CHUNKS) * CHUNK].reshape(TILE_CHUNKS, OCTETS, 8, SLAB)
        dst_ref[:, :, c0:c0 + TILE_CHUNKS, lanes] = _swap_outer_sublane(tile)
    for c in range(n_c - n_c % TILE_CHUNKS, n_c):
        for i in range(OCTETS):
            for s in range(8):
                r = c * CHUNK + 8 * i + s
                dst_ref[s, i, c:c + 1, lanes] = xs[r:r + 1]


def _from_grouped(src_ref, lanes, n_c, tail_ref):
    parts = []
    for c0 in range(0, n_c - n_c % TILE_CHUNKS, TILE_CHUNKS):
        tile = _swap_outer_sublane(src_ref[:, :, c0:c0 + TILE_CHUNKS, lanes])
        parts.append(tile.reshape(TILE_CHUNKS * CHUNK, SLAB))
    for k, c in enumerate(range(n_c - n_c % TILE_CHUNKS, n_c)):
        for i in range(OCTETS):
            for s in range(8):
                r = k * CHUNK + 8 * i + s
                tail_ref[r:r + 1, lanes] = src_ref[s, i, c:c + 1, lanes]
    if n_c % TILE_CHUNKS:
        parts.append(tail_ref[0:(n_c % TILE_CHUNKS) * CHUNK, lanes])
    xs = parts[0] if len(parts) == 1 else jnp.concatenate(parts, axis=0)
    return _transpose_blocks(xs)


GATE_COLS = 2 * D_MODEL


def _gates_u_kernel(hn_ref, w_ref, gates_ref, u_ref, *, n_chunk_rows):
    hn = hn_ref[...]
    u = _dot(hn, w_ref[:, GATE_COLS:])
    n_c = hn.shape[0] // CHUNK
    if n_chunk_rows > n_c:
        u_ref[:, :, n_c:, :] = jnp.zeros((8, OCTETS, n_chunk_rows - n_c, W_B), F32)
    n_gate = GATE_COLS // 1024
    slabs_per_gate = W_B // SLAB // n_gate
    for k in range(n_gate):
        cols = slice(k * 1024, (k + 1) * 1024)
        gates_ref[:, cols] = jax.nn.sigmoid(_dot(hn, w_ref[:, cols])).astype(BF16)
        for j in range(k * slabs_per_gate, (k + 1) * slabs_per_gate):
            lanes = slice(j * SLAB, (j + 1) * SLAB)
            _to_grouped(u[:, lanes], u_ref, lanes)


def _gates_u(hn, w_ug, *, tm):
    rows = hn.shape[0]
    n_tiles = rows // tm
    n_c = tm // CHUNK
    n_chunk_rows = n_c if n_tiles > 1 else -(-n_c // 16) * 16
    return pl.pallas_call(
        functools.partial(_gates_u_kernel, n_chunk_rows=n_chunk_rows),
        grid=(n_tiles,),
        in_specs=[
            pl.BlockSpec((tm, D_MODEL), lambda i: (i, 0)),
            _resident((D_MODEL, GATE_COLS + W_B)),
        ],
        out_specs=[pl.BlockSpec((tm, GATE_COLS), lambda i: (i, 0)),
                   pl.BlockSpec((8, OCTETS, n_chunk_rows, W_B), lambda i: (0, 0, i, 0))],
        out_shape=[jax.ShapeDtypeStruct((rows, GATE_COLS), BF16),
                   jax.ShapeDtypeStruct((8, OCTETS, n_tiles * n_chunk_rows, W_B), F32)],
        compiler_params=_params(1),
        name="gates_u",
    )(hn, w_ug)


def _ssm_group(u, lre_c, lim_c, ldt_c, lre_r, lim_r, ldt_r, bta, btb, ctr, cti, ddiag, h0,
               sel_lag, sel_ch, sel_row, t_ref, xs_ref):
    lane128 = lax.broadcasted_iota(jnp.int32, (1, 128), 1)
    sgn_conj = jnp.where(lane128 < SSM_P, 1.0, -1.0).astype(F32)
    sgn_mul = -sgn_conj

    def swap(h):
        return pltpu.roll(h, SSM_P, axis=1)

    def cmul(ar, ai, h):
        return ar * h + (ai * sgn_mul) * swap(h)

    dt_c = jnp.exp(ldt_c)
    k_row = lax.broadcasted_iota(jnp.int32, (1, 128), 1).astype(F32)
    mag = jnp.exp((lre_c * dt_c) * k_row)
    ang = (lim_c * dt_c) * k_row
    apt_re = mag * jnp.cos(ang)
    apt_im = mag * jnp.sin(ang)
    yield
    ae_re = _select_cols(apt_re, sel_lag)
    ae_im = _select_cols(apt_im, sel_lag)
    cte_re = _select_cols(ctr, sel_ch)
    cte_im = _select_cols(cti, sel_ch)
    ca_re = cte_re * ae_re - cte_im * ae_im
    ca_im = cte_re * ae_im + cte_im * ae_re
    a1_re = apt_re[:, 1:2]
    a1_im = apt_im[:, 1:2]
    ca1_re = ca_re * a1_re - ca_im * a1_im
    ca1_im = ca_re * a1_im + ca_im * a1_re
    yield

    dt_r = jnp.exp(ldt_r)
    xr = lre_r * dt_r
    xi = lim_r * dt_r
    a_re = jnp.exp(xr) * jnp.cos(xi)
    a_im = jnp.exp(xr) * jnp.sin(xi)
    den = lre_r * lre_r + lim_r * lim_r
    nr = a_re - 1.0
    cr = (nr * lre_r + a_im * lim_r) / den
    ci = (a_im * lre_r - nr * lim_r) / den
    bb_a = bta * cr + btb * ci
    bb_b = btb * cr - bta * ci

    k2 = _dot_hi(bb_a * sgn_conj, jnp.concatenate([ca_re, ca_im], axis=0))
    lane1k = lax.broadcasted_iota(jnp.int32, (SSM_H, CHUNK * SSM_H), 1)
    k2 = k2 + jnp.where(lane1k < SSM_H, _select_cols(ddiag, sel_ch), 0.0)
    k2z = jnp.concatenate([k2, jnp.zeros_like(k2)], axis=1)
    yield
    for r in range(8):
        kr = pltpu.roll(k2z, SSM_H * r, axis=1) if r else k2z
        for q in range(CHUNK // 8):
            m = 8 * q + r
            rows = slice(SSM_H * m, SSM_H * (m + 1))
            if q:
                t_ref[rows, 0:128 * q] = jnp.zeros((SSM_H, 128 * q), BF16)
            t_ref[rows, 128 * q:] = kr[:, 0:CHUNK * SSM_H - 128 * q].astype(BF16)
        if r % 4 == 3:
            yield

    e_col = (CHUNK - 1 - lax.broadcasted_iota(jnp.int32, (CHUNK, 1), 0)).astype(F32)
    apm = jnp.exp(e_col * xr)
    apa = e_col * xi
    apx_re = _select_rows(sel_row, apm * jnp.cos(apa))
    apx_im = _select_rows(sel_row, apm * jnp.sin(apa))
    tile_rows = lambda b: jnp.broadcast_to(b[None], (CHUNK, SSM_H, 128)).reshape(CHUNK * SSM_H, 128)
    bcat = (apx_re * tile_rows(bb_a) + apx_im * tile_rows(bb_b)).astype(BF16)
    ccat = jnp.concatenate([ca1_re, -ca1_im], axis=0).astype(BF16)
    yield

    s = _dot(u, bcat)
    al_re = jnp.exp(CHUNK * xr) * jnp.cos(CHUNK * xi)
    al_im = jnp.exp(CHUNK * xr) * jnp.sin(CHUNK * xi)

    h_meta = s[264:265]
    h_sample = cmul(al_re, al_im, h0) + s[256:264]
    yield

    n_p, n_c = 256, 128
    chunk = lax.broadcasted_iota(jnp.int32, (n_p, 128), 0) & (n_c - 1)
    x = s[0:n_p] + jnp.where(chunk == 0, cmul(al_re, al_im, h_meta), 0.0)
    head = n_c
    xs_ref[0:head, :] = jnp.zeros((head, 128), F32)
    ar, ai = al_re, al_im
    for step in range(7):
        sh = 1 << step
        xs_ref[head:head + n_p, :] = x
        x = x + jnp.where(chunk >= sh, cmul(ar, ai, xs_ref[head - sh:head + n_p - sh, :]), 0.0)
        ar, ai = ar * ar - ai * ai, 2.0 * ar * ai
        yield
    xs_ref[head:head + n_p, :] = x
    h_in_prompt = jnp.where(chunk == 0, h_meta, xs_ref[head - 1:head + n_p - 1, :])
    h_in = jnp.concatenate([h_in_prompt, h0, jnp.zeros((SSM_ROWS - n_p - 8, 128), F32)], axis=0)

    h_in16 = h_in.astype(BF16)
    y = jnp.concatenate(
        [_dot(u[:, 0:hi], t_ref[0:hi, hi - T_COLS:hi]) + _dot(h_in16, ccat[:, hi - T_COLS:hi])
         for hi in range(T_COLS, CHUNK * SSM_H + 1, T_COLS)], axis=1)
    h_fin = jnp.concatenate([x[n_c - 1:n_c], x[n_p - 1:n_p], jnp.zeros((6, 128), F32), h_sample], axis=0)
    return y, h_fin


def _interleave(tasks):
    live = list(tasks)
    while live:
        for task in list(live):
            try:
                next(task)
            except StopIteration:
                live.remove(task)


def _ssm_kernel(up_ref, us_ref, lre_c_ref, lim_c_ref, ldt_c_ref, lre_r_ref, lim_r_ref, ldt_r_ref,
                bta_ref, btb_ref, ctr_ref, cti_ref, dd_ref, h0_ref, yp_ref, ys_ref, hfin_ref,
                t_ref, xs_ref):
    kl = CHUNK * SSM_H
    lag_of_lane = lax.broadcasted_iota(jnp.int32, (128, kl), 1) >> 4
    sel_lag = (lax.broadcasted_iota(jnp.int32, (128, kl), 0) == lag_of_lane).astype(BF16)
    ch_of_lane = lax.broadcasted_iota(jnp.int32, (SSM_H, kl), 1) & (SSM_H - 1)
    sel_ch = (lax.broadcasted_iota(jnp.int32, (SSM_H, kl), 0) == ch_of_lane).astype(BF16)
    lag_of_row = lax.broadcasted_iota(jnp.int32, (kl, CHUNK), 0) >> 4
    sel_row = (lax.broadcasted_iota(jnp.int32, (kl, CHUNK), 1) == lag_of_row).astype(BF16)
    n_p = up_ref.shape[2]

    def group_task(s):
        u_p = jnp.concatenate([up_ref[s, i] for i in range(OCTETS)], axis=1)
        u_s = jnp.concatenate([us_ref[s, i] for i in range(OCTETS)], axis=1)
        u = jnp.concatenate([u_p, u_s], axis=0).astype(BF16)
        y, h_fin = yield from _ssm_group(
            u, lre_c_ref[s], lim_c_ref[s], ldt_c_ref[s], lre_r_ref[s], lim_r_ref[s],
            ldt_r_ref[s], bta_ref[s], btb_ref[s], ctr_ref[s], cti_ref[s], dd_ref[s], h0_ref[s],
            sel_lag, sel_ch, sel_row, t_ref.at[s % SSM_INTERLEAVE], xs_ref.at[s % SSM_INTERLEAVE])
        hfin_ref[s] = h_fin
        for i in range(OCTETS):
            lanes = slice(SLAB * i, SLAB * (i + 1))
            yp_ref[s, i] = y[0:n_p, lanes]
            ys_ref[s, i] = y[n_p:, lanes]

    for s in range(0, SLAB_GROUPS, SSM_INTERLEAVE):
        _interleave([group_task(s + k) for k in range(SSM_INTERLEAVE)])


def _ssm(u_p, u_s, lre_c, lim_c, ldt_c, lre_r, lim_r, ldt_r, bta, btb, ctr, cti, ddiag, h0):
    gb = SLAB_GROUPS
    per_group = lambda *tail: pl.BlockSpec((gb,) + tail, lambda j: (j,) + (0,) * len(tail))
    slab = lambda a: pl.BlockSpec(a.shape[:3] + (SLAB,), lambda j: (0, 0, 0, j))
    assert u_p.shape[2] + u_s.shape[2] == SSM_ROWS
    return pl.pallas_call(
        _ssm_kernel,
        grid=(SSM_G // gb,),
        in_specs=[
            slab(u_p), slab(u_s),
            per_group(SSM_P, 1), per_group(SSM_P, 1), per_group(SSM_P, 1),
            per_group(1, 128), per_group(1, 128), per_group(1, 128),
            per_group(SSM_H, 128), per_group(SSM_H, 128),
            per_group(SSM_P, SSM_H), per_group(SSM_P, SSM_H),
            per_group(SSM_H, SSM_H),
            per_group(8, 128),
        ],
        out_specs=[slab(u_p), slab(u_s), per_group(16, 128)],
        out_shape=[
            jax.ShapeDtypeStruct(u_p.shape, F32),
            jax.ShapeDtypeStruct(u_s.shape, F32),
            jax.ShapeDtypeStruct((SSM_G, 16, 128), F32),
        ],
        scratch_shapes=[
            pltpu.VMEM((SSM_INTERLEAVE, CHUNK * SSM_H, CHUNK * SSM_H), BF16),
            pltpu.VMEM((SSM_INTERLEAVE, 128 + 256, 128), F32),
        ],
        compiler_params=_params(1),
        name="ssm",
    )(u_p, u_s, lre_c, lim_c, ldt_c, lre_r, lim_r, ldt_r, bta, btb, ctr, cti, ddiag, h0)


def _merge_kernel(y_ref, outa_ref, sga_ref, sgb_ref, x_ref, gluw_ref, glub_ref, pa_ref,
                  pb_ref, wo_ref, x1_ref, yt_ref, merged_ref, tail_ref):
    tm = x_ref.shape[0]
    n_c = tm // CHUNK
    n_piece = 4
    width = D_MODEL // n_piece
    slabs_per_piece = W_B // SLAB // n_piece
    outa = outa_ref[...]
    proj = []
    for k in range(n_piece):
        proj.append(_dot(outa, pa_ref[:, k * width:(k + 1) * width]))
        for j in range(k * slabs_per_piece, (k + 1) * slabs_per_piece):
            lanes = slice(j * SLAB, (j + 1) * SLAB)
            yt_ref[:, lanes] = jax.nn.gelu(_from_grouped(y_ref, lanes, n_c, tail_ref))
    yb = yt_ref[...]
    glu = _dot(yb.astype(BF16), gluw_ref[...]) + glub_ref[...]
    out_b = (yb * jax.nn.sigmoid(glu)).astype(BF16)
    for k in range(n_piece):
        cols = slice(k * width, (k + 1) * width)
        merged_ref[:, cols] = (sga_ref[:, cols].astype(F32) * proj[k]
                               + sgb_ref[:, cols].astype(F32) * _dot(out_b, pb_ref[:, cols])).astype(BF16)
    x1_ref[...] = x_ref[...] + _dot(merged_ref[...], wo_ref[...])


def _merge(y_grouped, outa, ug, x, glu_w, glu_b, proj_a, proj_b, w_out, *, tm):
    rows = x.shape[0]
    n_tiles = rows // tm
    n_chunk_rows = y_grouped.shape[2] // n_tiles
    return pl.pallas_call(
        _merge_kernel,
        grid=(n_tiles,),
        in_specs=[
            pl.BlockSpec((8, OCTETS, n_chunk_rows, W_B), lambda i: (0, 0, i, 0)),
            pl.BlockSpec((tm, W_A), lambda i: (i, 0)),
            pl.BlockSpec((tm, D_MODEL), lambda i: (i, 0)),
            pl.BlockSpec((tm, D_MODEL), lambda i: (i, 1)),
            pl.BlockSpec((tm, D_MODEL), lambda i: (i, 0)),
            _resident((W_B, W_B)),
            _resident((1, W_B)),
            _resident((W_A, D_MODEL)),
            _resident((W_B, D_MODEL)),
            _resident((D_MODEL, D_MODEL)),
        ],
        out_specs=pl.BlockSpec((tm, D_MODEL), lambda i: (i, 0)),
        out_shape=jax.ShapeDtypeStruct((rows, D_MODEL), F32),
        scratch_shapes=[
            pltpu.VMEM((tm, W_B), F32),
            pltpu.VMEM((tm, D_MODEL), BF16),
            pltpu.VMEM((CHUNK * max(tm // CHUNK % TILE_CHUNKS, 1), W_B), F32),
        ],
        compiler_params=_params(1),
        name="merge",
    )(y_grouped, outa, ug, ug, x, glu_w, glu_b, proj_a, proj_b, w_out)


def _ffn_kernel(x1_ref, g_ref, wug_ref, wuv_ref, cw_all_ref, cb_all_ref, wd_ref,
                gfin_ref, initg_all_ref, initv_all_ref, y_ref, coutg_ref, coutv_ref,
                hn_ref, extg0_ref, extg1_ref, extv0_ref, extv1_ref, act0_ref, act1_ref,
                carryg_ref, carryv_ref, *, segs, chain_tiles, n_f, n_items):
    t = pl.program_id(0)
    item1 = jnp.clip(t - 1, 0, n_items - 1)
    item2 = jnp.clip(t - 2, 0, n_items - 1)
    f1, tile1 = item1 % n_f, item1 // n_f
    f2 = item2 % n_f
    draining = t >= 2
    cwg_ref, cwv_ref = cw_all_ref.at[f1], cw_all_ref.at[n_f + f1]
    bg_ref, bv_ref = cb_all_ref.at[f1], cb_all_ref.at[n_f + f1]
    initg_ref, initv_ref = initg_all_ref.at[f1], initv_all_ref.at[f1]

    @pl.when(t == 0)
    def _():
        extg1_ref[...] = jnp.zeros(extg1_ref.shape, F32)
        extv1_ref[...] = jnp.zeros(extv1_ref.shape, F32)
        act0_ref[...] = jnp.zeros(act0_ref.shape, BF16)
        y_ref[...] = jnp.zeros(y_ref.shape, F32)

    @pl.when(jnp.minimum(t, n_items - 1) % n_f == 0)
    def _():
        hn_ref[...] = _rmsnorm(x1_ref[...], g_ref[...]).astype(BF16)

    @pl.when(jnp.logical_and(f2 == 0, draining))
    def _():
        y_ref[...] = x1_ref[...]

    _start_stream(initg_ref, carryg_ref.at[f1], tile1, chain_tiles)
    _start_stream(initv_ref, carryv_ref.at[f1], tile1, chain_tiles)

    bufs = ((extg0_ref, extv0_ref, act0_ref), (extg1_ref, extv1_ref, act1_ref))
    for par in (0, 1):
        @pl.when(t % 2 == par)
        def _(par=par):
            up_g, up_v, act_in = bufs[par]
            conv_src_g, conv_src_v, act_out = bufs[1 - par]

            def up_piece(dst_ref, w_ref, cols):
                return lambda: _stash_rows(dst_ref, _dot(hn_ref[...], w_ref[:, cols]), segs, cols)

            def down_piece(cols):
                def run():
                    y_ref[:, cols] += _dot(act_in[...], wd_ref[:, cols])
                return run

            def gate_piece(k, s, rows, cols):
                def run():
                    base = s + CARRY_ROWS * k
                    g = _conv_piece(conv_src_g, cwg_ref, base, rows, cols) + bg_ref[:, cols]
                    v = _conv_piece(conv_src_v, cwv_ref, base, rows, cols) + bv_ref[:, cols]
                    act_out[s + rows[0]:s + rows[1], cols] = (jax.nn.silu(g) * v).astype(BF16)
                return run

            col_slices = lambda width, step: [slice(c, c + step) for c in range(0, width, step)]
            mxu = ([(up_piece(up_g, wug_ref, c), 4) for c in col_slices(FF_TILE, MXU_COLS)]
                   + [(up_piece(up_v, wuv_ref, c), 4) for c in col_slices(FF_TILE, MXU_COLS)]
                   + [(down_piece(c), 1) for c in col_slices(D_MODEL, MXU_COLS)])
            vpu = [gate_piece(k, s, (r, min(r + GATE_ROWS, n)), c)
                   for c in col_slices(FF_TILE, 128)
                   for k, (s, n) in enumerate(segs) for r in range(0, n, GATE_ROWS)]

            _conv_prepare(conv_src_g, initg_ref, coutg_ref.at[:, f1], carryg_ref.at[f1], segs, chain_tiles)
            _conv_prepare(conv_src_v, initv_ref, coutv_ref.at[:, f1], carryv_ref.at[f1], segs, chain_tiles)
            total = sum(weight for _, weight in mxu)
            done, issued = 0, 0
            for run, weight in mxu:
                run()
                done += weight
                while issued < len(vpu) and issued * total < done * len(vpu):
                    vpu[issued]()
                    issued += 1

    @pl.when(jnp.logical_and(f2 == n_f - 1, draining))
    def _():
        y_ref[...] = _rmsnorm(y_ref[...], gfin_ref[...])


def _ffn(x1, g, w_up, conv_w, conv_b, w_down, g_fin, init_g, init_v, *, tm, segs, chain_tiles):
    rows = x1.shape[0]
    n_tiles = rows // tm
    n_f = D_FF // FF_TILE
    n_items = n_tiles * n_f
    nseg = len(segs)
    n_out_seq = n_tiles // chain_tiles if chain_tiles else 1
    item = lambda t, lag: jnp.clip(t - lag, 0, n_items - 1)
    ff = lambda lag: (lambda t: item(t, lag) % n_f)
    tile = lambda lag: (lambda t: item(t, lag) // n_f)
    seq1 = lambda t: tile(1)(t) // chain_tiles if chain_tiles else 0
    cw_all = conv_w.reshape(3, 2 * n_f, FF_TILE).transpose(1, 0, 2)
    cb_all = conv_b.reshape(2 * n_f, 1, FF_TILE)
    per_tile = lambda a: a.reshape(a.shape[0], CARRY_ROWS, n_f, FF_TILE).transpose(2, 0, 1, 3)
    init_g, init_v = per_tile(init_g), per_tile(init_v)
    cout_block = (nseg, n_f, CARRY_ROWS, FF_TILE)
    cout_shape = jax.ShapeDtypeStruct((n_out_seq * nseg, n_f, CARRY_ROWS, FF_TILE), F32)
    ext_shape = (tm + CARRY_ROWS * nseg, FF_TILE)
    y, cout_g, cout_v = pl.pallas_call(
        functools.partial(_ffn_kernel, segs=segs, chain_tiles=chain_tiles, n_f=n_f, n_items=n_items),
        grid=(n_items + 2,),
        in_specs=[
            pl.BlockSpec((tm, D_MODEL), lambda t: (tile(0)(t), 0)),
            pl.BlockSpec((1, D_MODEL), lambda t: (0, 0)),
            pl.BlockSpec((D_MODEL, FF_TILE), lambda t: (0, ff(0)(t))),
            pl.BlockSpec((D_MODEL, FF_TILE), lambda t: (0, n_f + ff(0)(t))),
            _resident(cw_all.shape),
            _resident(cb_all.shape),
            pl.BlockSpec((FF_TILE, D_MODEL), lambda t: (ff(2)(t), 0)),
            pl.BlockSpec((1, D_MODEL), lambda t: (0, 0)),
            _resident(init_g.shape),
            _resident(init_v.shape),
        ],
        out_specs=[
            pl.BlockSpec((tm, D_MODEL), lambda t: (tile(2)(t), 0)),
            pl.BlockSpec(cout_block, lambda t: (seq1(t), 0, 0, 0)),
            pl.BlockSpec(cout_block, lambda t: (seq1(t), 0, 0, 0)),
        ],
        out_shape=[jax.ShapeDtypeStruct((rows, D_MODEL), F32), cout_shape, cout_shape],
        scratch_shapes=[
            pltpu.VMEM((tm, D_MODEL), BF16),
            pltpu.VMEM(ext_shape, F32), pltpu.VMEM(ext_shape, F32),
            pltpu.VMEM(ext_shape, F32), pltpu.VMEM(ext_shape, F32),
            pltpu.VMEM((tm, FF_TILE), BF16), pltpu.VMEM((tm, FF_TILE), BF16),
            pltpu.VMEM((n_f, CARRY_ROWS, FF_TILE), F32),
            pltpu.VMEM((n_f, CARRY_ROWS, FF_TILE), F32),
        ],
        compiler_params=_params(1),
        name="ffn",
    )(x1, g, w_up, w_up, cw_all, cb_all, w_down, g_fin, init_g, init_v)
    widen = lambda c: c.transpose(0, 2, 1, 3).reshape(c.shape[0], CARRY_ROWS, D_FF)
    return y, widen(cout_g), widen(cout_v)


def _history_block(buf):
    return jnp.pad(buf, ((0, 0), (CARRY_ROWS - buf.shape[1], 0), (0, 0)))


def kernel(x_prompt, x_sample, cache_conv_a, state_ssm_re, state_ssm_im, cache_ffn_conv,
           meta_tokens, norm_mix_g, w_in, conv_a_w, ssm_lambda_re, ssm_lambda_im, ssm_log_dt,
           ssm_b_re, ssm_b_im, ssm_c_re, ssm_c_im, ssm_d, glu_w, glu_b, proj_a, proj_b,
           w_out, norm_ffn_g, w_up, ffn_conv_w, ffn_conv_b, w_down, norm_final_g):
    batch, seq, _ = x_prompt.shape
    dec_batch, dec_seq, _ = x_sample.shape
    assert dec_seq == CHUNK and seq % CHUNK == 0 and seq % ROW_TILE == 0 and seq % FFN_ROW_TILE == 0
    n_chunks = seq // CHUNK
    assert batch * n_chunks == 256 and dec_batch == 8

    w_a = w_in[0, :, :3 * W_A].astype(BF16)
    w_ug = jnp.concatenate([w_in[0, :, 3 * W_A + W_B:], w_in[0, :, 3 * W_A:3 * W_A + W_B]],
                           axis=1).astype(BF16)
    glu_w16, proj_a16, proj_b16 = glu_w[0].astype(BF16), proj_a[0].astype(BF16), proj_b[0].astype(BF16)
    w_out16, w_up16, w_down16 = w_out[0].astype(BF16), w_up[0].astype(BF16), w_down[0].astype(BF16)
    g_mix, g_ffn, g_fin = norm_mix_g[0][None], norm_ffn_g[0][None], norm_final_g[None]
    glu_b2, ffn_b2 = glu_b[0][None], ffn_conv_b[0][None]

    n_s = dec_batch * dec_seq
    n_pad = CHUNK - N_META
    rows_s = n_s + CHUNK
    x_s = jnp.concatenate([x_sample.reshape(n_s, D_MODEL), jnp.zeros((n_pad, D_MODEL), F32), meta_tokens],
                          axis=0)
    segs_s = (tuple((b * dec_seq, dec_seq) for b in range(dec_batch))
              + ((n_s, n_pad), (n_s + n_pad, N_META)))
    meta_seg = dec_batch + 1
    x_p = x_prompt.reshape(batch * seq, D_MODEL)
    segs_p = ((0, ROW_TILE),)
    chain = seq // ROW_TILE

    zero_hist = lambda c: jnp.zeros((1, CARRY_ROWS, c), F32)
    init_a_s = jnp.concatenate([_history_block(cache_conv_a[0]), zero_hist(W_A), zero_hist(W_A)], axis=0)
    hn_s, outa_s, ca_s = _mixer_a(x_s, g_mix, w_a, conv_a_w[0], init_a_s,
                                  tm=rows_s, segs=segs_s, chain_tiles=0)
    hn_p, outa_p, ca_p = _mixer_a(x_p, g_mix, w_a, conv_a_w[0], ca_s[meta_seg:],
                                  tm=ROW_TILE, segs=segs_p, chain_tiles=chain)
    gates_s, u_s = _gates_u(hn_s, w_ug, tm=rows_s)
    gates_p, u_p = _gates_u(hn_p, w_ug, tm=ROW_TILE)

    dup = lambda v: jnp.concatenate([v, v], axis=-1)[:, None, :]
    lre, lim = ssm_lambda_re[0], ssm_lambda_im[0]
    ldt = jnp.broadcast_to(ssm_log_dt[0][:, None], (SSM_G, SSM_P))
    bt_re, bt_im = ssm_b_re[0].swapaxes(1, 2), ssm_b_im[0].swapaxes(1, 2)
    bta = jnp.concatenate([bt_re, bt_im], axis=-1)
    btb = jnp.concatenate([-bt_im, bt_re], axis=-1)
    ddiag = ssm_d[0][:, :, None] * jnp.eye(SSM_H, dtype=F32)
    h0 = jnp.concatenate([state_ssm_re[0], state_ssm_im[0]], axis=-1).swapaxes(0, 1)
    yb_p, yb_s, h_fin = _ssm(u_p, u_s, lre[:, :, None], lim[:, :, None], ldt[:, :, None],
                             dup(lre), dup(lim), dup(ldt), bta, btb,
                             ssm_c_re[0].swapaxes(1, 2), ssm_c_im[0].swapaxes(1, 2), ddiag, h0)

    x1_s = _merge(yb_s, outa_s, gates_s, x_s, glu_w16, glu_b2, proj_a16, proj_b16, w_out16, tm=rows_s)
    x1_p = _merge(yb_p, outa_p, gates_p, x_p, glu_w16, glu_b2, proj_a16, proj_b16, w_out16, tm=ROW_TILE)

    hist_f = _history_block(cache_ffn_conv[0])
    init_g_s = jnp.concatenate([hist_f[:, :, :D_FF], zero_hist(D_FF), zero_hist(D_FF)], axis=0)
    init_v_s = jnp.concatenate([hist_f[:, :, D_FF:], zero_hist(D_FF), zero_hist(D_FF)], axis=0)
    y_s, cg_s, cv_s = _ffn(x1_s, g_ffn, w_up16, ffn_conv_w[0], ffn_b2, w_down16, g_fin,
                           init_g_s, init_v_s, tm=rows_s, segs=segs_s, chain_tiles=0)
    y_p, cg_p, cv_p = _ffn(x1_p, g_ffn, w_up16, ffn_conv_w[0], ffn_b2, w_down16, g_fin,
                           cg_s[meta_seg:], cv_s[meta_seg:], tm=FFN_ROW_TILE,
                           segs=((0, FFN_ROW_TILE),), chain_tiles=seq // FFN_ROW_TILE)

    hist = lambda c: c[:, CARRY_ROWS - 2:][None]
    ffn_hist = lambda cg, cv: hist(jnp.concatenate([cg, cv], axis=-1))
    state = lambda h: (h[:, :, :SSM_P].swapaxes(0, 1)[None], h[:, :, SSM_P:].swapaxes(0, 1)[None])
    p_re, p_im = state(h_fin[:, 0:batch])
    s_re, s_im = state(h_fin[:, 8:8 + dec_batch])
    return (y_p.reshape(batch, seq, D_MODEL), y_s[:n_s].reshape(dec_batch, dec_seq, D_MODEL),
            hist(ca_p), p_re, p_im, ffn_hist(cg_p, cv_p),
            hist(ca_s[:dec_batch]), s_re, s_im, ffn_hist(cg_s[:dec_batch], cv_s[:dec_batch]))
```

```python
import functools

import jax
import jax.numpy as jnp
from jax import lax
from jax.experimental import pallas as pl
from jax.experimental.pallas import tpu as pltpu

D_MODEL = 2048
W_A = 1024
W_B = 1024
SSM_H = 16
SSM_G = 64
SSM_P = 64
D_FF = 5632
CHUNK = 64
N_META = 16
EPS = 1e-6

BF16 = jnp.bfloat16
F32 = jnp.float32
HIGHEST = lax.Precision.HIGHEST

CARRY_ROWS = 8
ROW_TILE = 512
FFN_ROW_TILE = 512
FF_TILE = 512
FF_TILE_SHORT = 256
MXU_COLS = 256
GATE_ROWS = 64
SSM_INTERLEAVE = 4
T_COLS = 256
SLAB = 128
SLAB_GROUPS = SLAB // SSM_H
SSM_ROWS = 272
VMEM_LIMIT = 56 * 1024 * 1024


def _dot(a, b):
    return jnp.dot(a, b, preferred_element_type=F32)


def _dot_hi(a, b):
    return jnp.dot(a, b, preferred_element_type=F32, precision=HIGHEST)


def _split3(x):
    hi = x.astype(BF16)
    rest = x - hi.astype(F32)
    mid = rest.astype(BF16)
    return hi, mid, (rest - mid.astype(F32)).astype(BF16)


def _select_cols(x, sel):
    return sum(_dot(part, sel) for part in _split3(x))


def _select_rows(sel, x):
    return sum(_dot(sel, part) for part in _split3(x))


def _rmsnorm(x, g):
    ms = jnp.mean(x * x, axis=-1, keepdims=True)
    return (x * lax.rsqrt(ms + EPS)) * g


def _params(n_axes):
    return pltpu.CompilerParams(
        dimension_semantics=("arbitrary",) * n_axes, vmem_limit_bytes=VMEM_LIMIT)


def _resident(shape):
    return pl.BlockSpec(shape, lambda *_: (0,) * len(shape), pipeline_mode=pl.Buffered(1))


def _stash_rows(ext_ref, v, segs, cols=slice(None)):
    for k, (s, n) in enumerate(segs):
        base = s + CARRY_ROWS * (k + 1)
        ext_ref[base:base + n, cols] = v[s:s + n]


def _conv_prepare(ext_ref, init_ref, cout_ref, carry_ref, segs, chain_tiles):
    for k, (s, n) in enumerate(segs):
        base = s + CARRY_ROWS * k
        ext_ref[base:base + CARRY_ROWS, :] = carry_ref[...] if chain_tiles else init_ref[k]
        last = ext_ref[base + n:base + n + CARRY_ROWS, :]
        cout_ref[k] = last
        if chain_tiles:
            carry_ref[...] = last


def _conv_piece(ext_ref, w_ref, base, rows, cols=slice(None)):
    r0, r1 = rows
    t2 = ext_ref[base + 6 + r0:base + 6 + r1, cols]
    t1 = ext_ref[base + 7 + r0:base + 7 + r1, cols]
    t0 = ext_ref[base + 8 + r0:base + 8 + r1, cols]
    return t2 * w_ref[0:1, cols] + t1 * w_ref[1:2, cols] + t0 * w_ref[2:3, cols]


def _start_stream(init_ref, carry_ref, tile, chain_tiles):
    if chain_tiles:
        @pl.when(tile % chain_tiles == 0)
        def _():
            carry_ref[...] = init_ref[0]


def _mixer_a_kernel(x_ref, g_ref, w_ref, cw_ref, init_ref, hn_ref, outa_ref, cout_ref,
                    ext_ref, carry_ref, *, segs, chain_tiles):
    _start_stream(init_ref, carry_ref, pl.program_id(0), chain_tiles)
    hn = _rmsnorm(x_ref[...], g_ref[...]).astype(BF16)
    hn_ref[...] = hn
    zb = _dot(hn, w_ref[:, 0:W_A])
    zc = _dot(hn, w_ref[:, W_A:2 * W_A])
    zh = _dot(hn, w_ref[:, 2 * W_A:3 * W_A])
    _stash_rows(ext_ref, zc * zh, segs)
    _conv_prepare(ext_ref, init_ref, cout_ref, carry_ref, segs, chain_tiles)
    for k, (s, n) in enumerate(segs):
        conv = _conv_piece(ext_ref, cw_ref, s + CARRY_ROWS * k, (0, n))
        outa_ref[s:s + n, :] = (zb[s:s + n] * conv).astype(BF16)


def _mixer_a(x, g, w_a, conv_w, init, *, tm, segs, chain_tiles):
    rows = x.shape[0]
    n_tiles = rows // tm
    nseg = len(segs)
    n_out_seq = n_tiles // chain_tiles if chain_tiles else 1
    seq = (lambda i: (i // chain_tiles, 0, 0)) if chain_tiles else (lambda i: (0, 0, 0))
    return pl.pallas_call(
        functools.partial(_mixer_a_kernel, segs=segs, chain_tiles=chain_tiles),
        grid=(n_tiles,),
        in_specs=[
            pl.BlockSpec((tm, D_MODEL), lambda i: (i, 0)),
            _resident((1, D_MODEL)),
            _resident((D_MODEL, 3 * W_A)),
            _resident((3, W_A)),
            pl.BlockSpec((nseg, CARRY_ROWS, W_A), lambda i: (0, 0, 0)),
        ],
        out_specs=[
            pl.BlockSpec((tm, D_MODEL), lambda i: (i, 0)),
            pl.BlockSpec((tm, W_A), lambda i: (i, 0)),
            pl.BlockSpec((nseg, CARRY_ROWS, W_A), seq),
        ],
        out_shape=[
            jax.ShapeDtypeStruct((rows, D_MODEL), BF16),
            jax.ShapeDtypeStruct((rows, W_A), BF16),
            jax.ShapeDtypeStruct((n_out_seq * nseg, CARRY_ROWS, W_A), F32),
        ],
        scratch_shapes=[
            pltpu.VMEM((tm + CARRY_ROWS * nseg, W_A), F32),
            pltpu.VMEM((CARRY_ROWS, W_A), F32),
        ],
        compiler_params=_params(1),
        name="mixer_a",
    )(x, g, w_a, conv_w, init)


def _transpose_blocks(x):
    rows = x.shape[0]
    shape3 = (rows // 8, 8, SLAB)
    blk = lax.broadcasted_iota(jnp.int32, shape3, 2) >> 4
    x = pltpu.roll(x, 0, 1, stride=SSM_H, stride_axis=0)
    x3 = x.reshape(shape3)
    for d in (4, 2, 1):
        x3 = jnp.where((blk & d) != 0, pltpu.roll(x3, 8 - d, 1), x3)
    x3 = pltpu.roll(x3.reshape(rows, SLAB), 0, 1, stride=SSM_H, stride_axis=0).reshape(shape3)
    r = lax.broadcasted_iota(jnp.int32, shape3, 1) & 3
    x3 = jnp.where(r == 0, x3, jnp.where(r == 1, pltpu.roll(x3, 2, 1),
                                        jnp.where(r == 2, pltpu.roll(x3, 4, 1), pltpu.roll(x3, 6, 1))))
    return x3.reshape(rows, SLAB)


def _swap_outer_sublane(a):
    p = lax.broadcasted_iota(jnp.int32, a.shape, 0)
    q = lax.broadcasted_iota(jnp.int32, a.shape, 2)
    for d in (4, 2, 1):
        p_hi, q_hi = (p & d) != 0, (q & d) != 0
        up = pltpu.roll(jnp.roll(a, d, axis=0), 8 - d, 2)
        dn = pltpu.roll(jnp.roll(a, -d, axis=0), d, 2)
        a = jnp.where(jnp.logical_and(p_hi, jnp.logical_not(q_hi)), up,
                      jnp.where(jnp.logical_and(jnp.logical_not(p_hi), q_hi), dn, a))
    return a


OCTETS = CHUNK // 8
TILE_CHUNKS = 8


def _to_grouped(x, dst_ref, lanes):
    xs = _transpose_blocks(x)
    n_c = x.shape[0] // CHUNK
    for c0 in range(0, n_c - n_c % TILE_CHUNKS, TILE_CHUNKS):
        tile = xs[c0 * CHUNK:(c0 + TILE_CHUNKS) * CHUNK].reshape(TILE_CHUNKS, OCTETS, 8, SLAB)
        dst_ref[:, :, c0:c0 + TILE_CHUNKS, lanes] = _swap_outer_sublane(tile)
    for c in range(n_c - n_c % TILE_CHUNKS, n_c):
        for i in range(OCTETS):
            for s in range(8):
                r = c * CHUNK + 8 * i + s
                dst_ref[s, i, c:c + 1, lanes] = xs[r:r + 1]


def _from_grouped(src_ref, lanes, n_c, tail_ref):
    parts = []
    for c0 in range(0, n_c - n_c % TILE_CHUNKS, TILE_CHUNKS):
        tile = _swap_outer_sublane(src_ref[:, :, c0:c0 + TILE_CHUNKS, lanes])
        parts.append(tile.reshape(TILE_CHUNKS * CHUNK, SLAB))
    for k, c in enumerate(range(n_c - n_c % TILE_CHUNKS, n_c)):
        for i in range(OCTETS):
            for s in range(8):
                r = k * CHUNK + 8 * i + s
                tail_ref[r:r + 1, lanes] = src_ref[s, i, c:c + 1, lanes]
    if n_c % TILE_CHUNKS:
        parts.append(tail_ref[0:(n_c % TILE_CHUNKS) * CHUNK, lanes])
    xs = parts[0] if len(parts) == 1 else jnp.concatenate(parts, axis=0)
    return _transpose_blocks(xs)


GATE_COLS = 2 * D_MODEL


def _gates_u_kernel(hn_ref, w_ref, gates_ref, u_ref, *, n_chunk_rows):
    hn = hn_ref[...]
    u = _dot(hn, w_ref[:, GATE_COLS:])
    n_c = hn.shape[0] // CHUNK
    if n_chunk_rows > n_c:
        u_ref[:, :, n_c:, :] = jnp.zeros((8, OCTETS, n_chunk_rows - n_c, W_B), F32)
    n_gate = GATE_COLS // 1024
    slabs_per_gate = W_B // SLAB // n_gate
    for k in range(n_gate):
        cols = slice(k * 1024, (k + 1) * 1024)
        gates_ref[:, cols] = jax.nn.sigmoid(_dot(hn, w_ref[:, cols])).astype(BF16)
        for j in range(k * slabs_per_gate, (k + 1) * slabs_per_gate):
            lanes = slice(j * SLAB, (j + 1) * SLAB)
            _to_grouped(u[:, lanes], u_ref, lanes)


def _gates_u(hn, w_ug, *, tm):
    rows = hn.shape[0]
    n_tiles = rows // tm
    n_c = tm // CHUNK
    n_chunk_rows = n_c if n_tiles > 1 else -(-n_c // 16) * 16
    return pl.pallas_call(
        functools.partial(_gates_u_kernel, n_chunk_rows=n_chunk_rows),
        grid=(n_tiles,),
        in_specs=[
            pl.BlockSpec((tm, D_MODEL), lambda i: (i, 0)),
            _resident((D_MODEL, GATE_COLS + W_B)),
        ],
        out_specs=[pl.BlockSpec((tm, GATE_COLS), lambda i: (i, 0)),
                   pl.BlockSpec((8, OCTETS, n_chunk_rows, W_B), lambda i: (0, 0, i, 0))],
        out_shape=[jax.ShapeDtypeStruct((rows, GATE_COLS), BF16),
                   jax.ShapeDtypeStruct((8, OCTETS, n_tiles * n_chunk_rows, W_B), F32)],
        compiler_params=_params(1),
        name="gates_u",
    )(hn, w_ug)


def _ssm_group(u, lre_c, lim_c, ldt_c, lre_r, lim_r, ldt_r, bta, btb, ctr, cti, ddiag, h0,
               sel_lag, sel_ch, sel_row, t_ref, xs_ref):
    lane128 = lax.broadcasted_iota(jnp.int32, (1, 128), 1)
    sgn_conj = jnp.where(lane128 < SSM_P, 1.0, -1.0).astype(F32)
    sgn_mul = -sgn_conj

    def swap(h):
        return pltpu.roll(h, SSM_P, axis=1)

    def cmul(ar, ai, h):
        return ar * h + (ai * sgn_mul) * swap(h)

    dt_c = jnp.exp(ldt_c)
    k_row = lax.broadcasted_iota(jnp.int32, (1, 128), 1).astype(F32)
    mag = jnp.exp((lre_c * dt_c) * k_row)
    ang = (lim_c * dt_c) * k_row
    apt_re = mag * jnp.cos(ang)
    apt_im = mag * jnp.sin(ang)
    yield
    ae_re = _select_cols(apt_re, sel_lag)
    ae_im = _select_cols(apt_im, sel_lag)
    cte_re = _select_cols(ctr, sel_ch)
    cte_im = _select_cols(cti, sel_ch)
    ca_re = cte_re * ae_re - cte_im * ae_im
    ca_im = cte_re * ae_im + cte_im * ae_re
    a1_re = apt_re[:, 1:2]
    a1_im = apt_im[:, 1:2]
    ca1_re = ca_re * a1_re - ca_im * a1_im
    ca1_im = ca_re * a1_im + ca_im * a1_re
    yield

    dt_r = jnp.exp(ldt_r)
    xr = lre_r * dt_r
    xi = lim_r * dt_r
    a_re = jnp.exp(xr) * jnp.cos(xi)
    a_im = jnp.exp(xr) * jnp.sin(xi)
    den = lre_r * lre_r + lim_r * lim_r
    nr = a_re - 1.0
    cr = (nr * lre_r + a_im * lim_r) / den
    ci = (a_im * lre_r - nr * lim_r) / den
    bb_a = bta * cr + btb * ci
    bb_b = btb * cr - bta * ci

    k2 = _dot_hi(bb_a * sgn_conj, jnp.concatenate([ca_re, ca_im], axis=0))
    lane1k = lax.broadcasted_iota(jnp.int32, (SSM_H, CHUNK * SSM_H), 1)
    k2 = k2 + jnp.where(lane1k < SSM_H, _select_cols(ddiag, sel_ch), 0.0)
    k2z = jnp.concatenate([k2, jnp.zeros_like(k2)], axis=1)
    yield
    for r in range(8):
        kr = pltpu.roll(k2z, SSM_H * r, axis=1) if r else k2z
        for q in range(CHUNK // 8):
            m = 8 * q + r
            rows = slice(SSM_H * m, SSM_H * (m + 1))
            if q:
                t_ref[rows, 0:128 * q] = jnp.zeros((SSM_H, 128 * q), BF16)
            t_ref[rows, 128 * q:] = kr[:, 0:CHUNK * SSM_H - 128 * q].astype(BF16)
        if r % 4 == 3:
            yield

    e_col = (CHUNK - 1 - lax.broadcasted_iota(jnp.int32, (CHUNK, 1), 0)).astype(F32)
    apm = jnp.exp(e_col * xr)
    apa = e_col * xi
    apx_re = _select_rows(sel_row, apm * jnp.cos(apa))
    apx_im = _select_rows(sel_row, apm * jnp.sin(apa))
    tile_rows = lambda b: jnp.broadcast_to(b[None], (CHUNK, SSM_H, 128)).reshape(CHUNK * SSM_H, 128)
    bcat = (apx_re * tile_rows(bb_a) + apx_im * tile_rows(bb_b)).astype(BF16)
    ccat = jnp.concatenate([ca1_re, -ca1_im], axis=0).astype(BF16)
    yield

    s = _dot(u, bcat)
    al_re = jnp.exp(CHUNK * xr) * jnp.cos(CHUNK * xi)
    al_im = jnp.exp(CHUNK * xr) * jnp.sin(CHUNK * xi)

    h_meta = s[264:265]
    h_sample = cmul(al_re, al_im, h0) + s[256:264]
    yield

    n_p, n_c = 256, 128
    chunk = lax.broadcasted_iota(jnp.int32, (n_p, 128), 0) & (n_c - 1)
    x = s[0:n_p] + jnp.where(chunk == 0, cmul(al_re, al_im, h_meta), 0.0)
    head = n_c
    xs_ref[0:head, :] = jnp.zeros((head, 128), F32)
    ar, ai = al_re, al_im
    for step in range(7):
        sh = 1 << step
        xs_ref[head:head + n_p, :] = x
        x = x + jnp.where(chunk >= sh, cmul(ar, ai, xs_ref[head - sh:head + n_p - sh, :]), 0.0)
        ar, ai = ar * ar - ai * ai, 2.0 * ar * ai
        yield
    xs_ref[head:head + n_p, :] = x
    h_in_prompt = jnp.where(chunk == 0, h_meta, xs_ref[head - 1:head + n_p - 1, :])
    h_in = jnp.concatenate([h_in_prompt, h0, jnp.zeros((SSM_ROWS - n_p - 8, 128), F32)], axis=0)

    h_in16 = h_in.astype(BF16)
    y = jnp.concatenate(
        [_dot(u[:, 0:hi], t_ref[0:hi, hi - T_COLS:hi]) + _dot(h_in16, ccat[:, hi - T_COLS:hi])
         for hi in range(T_COLS, CHUNK * SSM_H + 1, T_COLS)], axis=1)
    h_fin = jnp.concatenate([x[n_c - 1:n_c], x[n_p - 1:n_p], jnp.zeros((6, 128), F32), h_sample], axis=0)
    return y, h_fin


def _interleave(tasks):
    live = list(tasks)
    while live:
        for task in list(live):
            try:
                next(task)
            except StopIteration:
                live.remove(task)


def _ssm_kernel(up_ref, us_ref, lre_c_ref, lim_c_ref, ldt_c_ref, lre_r_ref, lim_r_ref, ldt_r_ref,
                bta_ref, btb_ref, ctr_ref, cti_ref, dd_ref, h0_ref, yp_ref, ys_ref, hfin_ref,
                t_ref, xs_ref):
    kl = CHUNK * SSM_H
    lag_of_lane = lax.broadcasted_iota(jnp.int32, (128, kl), 1) >> 4
    sel_lag = (lax.broadcasted_iota(jnp.int32, (128, kl), 0) == lag_of_lane).astype(BF16)
    ch_of_lane = lax.broadcasted_iota(jnp.int32, (SSM_H, kl), 1) & (SSM_H - 1)
    sel_ch = (lax.broadcasted_iota(jnp.int32, (SSM_H, kl), 0) == ch_of_lane).astype(BF16)
    lag_of_row = lax.broadcasted_iota(jnp.int32, (kl, CHUNK), 0) >> 4
    sel_row = (lax.broadcasted_iota(jnp.int32, (kl, CHUNK), 1) == lag_of_row).astype(BF16)
    n_p = up_ref.shape[2]

    def group_task(s):
        u_p = jnp.concatenate([up_ref[s, i] for i in range(OCTETS)], axis=1)
        u_s = jnp.concatenate([us_ref[s, i] for i in range(OCTETS)], axis=1)
        u = jnp.concatenate([u_p, u_s], axis=0).astype(BF16)
        y, h_fin = yield from _ssm_group(
            u, lre_c_ref[s], lim_c_ref[s], ldt_c_ref[s], lre_r_ref[s], lim_r_ref[s],
            ldt_r_ref[s], bta_ref[s], btb_ref[s], ctr_ref[s], cti_ref[s], dd_ref[s], h0_ref[s],
            sel_lag, sel_ch, sel_row, t_ref.at[s % SSM_INTERLEAVE], xs_ref.at[s % SSM_INTERLEAVE])
        hfin_ref[s] = h_fin
        for i in range(OCTETS):
            lanes = slice(SLAB * i, SLAB * (i + 1))
            yp_ref[s, i] = y[0:n_p, lanes]
            ys_ref[s, i] = y[n_p:, lanes]

    for s in range(0, SLAB_GROUPS, SSM_INTERLEAVE):
        _interleave([group_task(s + k) for k in range(SSM_INTERLEAVE)])


def _ssm(u_p, u_s, lre_c, lim_c, ldt_c, lre_r, lim_r, ldt_r, bta, btb, ctr, cti, ddiag, h0):
    gb = SLAB_GROUPS
    per_group = lambda *tail: pl.BlockSpec((gb,) + tail, lambda j: (j,) + (0,) * len(tail))
    slab = lambda a: pl.BlockSpec(a.shape[:3] + (SLAB,), lambda j: (0, 0, 0, j))
    assert u_p.shape[2] + u_s.shape[2] == SSM_ROWS
    return pl.pallas_call(
        _ssm_kernel,
        grid=(SSM_G // gb,),
        in_specs=[
            slab(u_p), slab(u_s),
            per_group(SSM_P, 1), per_group(SSM_P, 1), per_group(SSM_P, 1),
            per_group(1, 128), per_group(1, 128), per_group(1, 128),
            per_group(SSM_H, 128), per_group(SSM_H, 128),
            per_group(SSM_P, SSM_H), per_group(SSM_P, SSM_H),
            per_group(SSM_H, SSM_H),
            per_group(8, 128),
        ],
        out_specs=[slab(u_p), slab(u_s), per_group(16, 128)],
        out_shape=[
            jax.ShapeDtypeStruct(u_p.shape, F32),
            jax.ShapeDtypeStruct(u_s.shape, F32),
            jax.ShapeDtypeStruct((SSM_G, 16, 128), F32),
        ],
        scratch_shapes=[
            pltpu.VMEM((SSM_INTERLEAVE, CHUNK * SSM_H, CHUNK * SSM_H), BF16),
            pltpu.VMEM((SSM_INTERLEAVE, 128 + 256, 128), F32),
        ],
        compiler_params=_params(1),
        name="ssm",
    )(u_p, u_s, lre_c, lim_c, ldt_c, lre_r, lim_r, ldt_r, bta, btb, ctr, cti, ddiag, h0)


def _merge_kernel(y_ref, outa_ref, sga_ref, sgb_ref, x_ref, gluw_ref, glub_ref, pa_ref,
                  pb_ref, wo_ref, x1_ref, yt_ref, merged_ref, tail_ref):
    tm = x_ref.shape[0]
    n_c = tm // CHUNK
    n_piece = 4
    width = D_MODEL // n_piece
    slabs_per_piece = W_B // SLAB // n_piece
    outa = outa_ref[...]
    proj = []
    for k in range(n_piece):
        proj.append(_dot(outa, pa_ref[:, k * width:(k + 1) * width]))
        for j in range(k * slabs_per_piece, (k + 1) * slabs_per_piece):
            lanes = slice(j * SLAB, (j + 1) * SLAB)
            yt_ref[:, lanes] = jax.nn.gelu(_from_grouped(y_ref, lanes, n_c, tail_ref))
    yb = yt_ref[...]
    glu = _dot(yb.astype(BF16), gluw_ref[...]) + glub_ref[...]
    out_b = (yb * jax.nn.sigmoid(glu)).astype(BF16)
    for k in range(n_piece):
        cols = slice(k * width, (k + 1) * width)
        merged_ref[:, cols] = (sga_ref[:, cols].astype(F32) * proj[k]
                               + sgb_ref[:, cols].astype(F32) * _dot(out_b, pb_ref[:, cols])).astype(BF16)
    x1_ref[...] = x_ref[...] + _dot(merged_ref[...], wo_ref[...])


def _merge(y_grouped, outa, ug, x, glu_w, glu_b, proj_a, proj_b, w_out, *, tm):
    rows = x.shape[0]
    n_tiles = rows // tm
    n_chunk_rows = y_grouped.shape[2] // n_tiles
    return pl.pallas_call(
        _merge_kernel,
        grid=(n_tiles,),
        in_specs=[
            pl.BlockSpec((8, OCTETS, n_chunk_rows, W_B), lambda i: (0, 0, i, 0)),
            pl.BlockSpec((tm, W_A), lambda i: (i, 0)),
            pl.BlockSpec((tm, D_MODEL), lambda i: (i, 0)),
            pl.BlockSpec((tm, D_MODEL), lambda i: (i, 1)),
            pl.BlockSpec((tm, D_MODEL), lambda i: (i, 0)),
            _resident((W_B, W_B)),
            _resident((1, W_B)),
            _resident((W_A, D_MODEL)),
            _resident((W_B, D_MODEL)),
            _resident((D_MODEL, D_MODEL)),
        ],
        out_specs=pl.BlockSpec((tm, D_MODEL), lambda i: (i, 0)),
        out_shape=jax.ShapeDtypeStruct((rows, D_MODEL), F32),
        scratch_shapes=[
            pltpu.VMEM((tm, W_B), F32),
            pltpu.VMEM((tm, D_MODEL), BF16),
            pltpu.VMEM((CHUNK * max(tm // CHUNK % TILE_CHUNKS, 1), W_B), F32),
        ],
        compiler_params=_params(1),
        name="merge",
    )(y_grouped, outa, ug, ug, x, glu_w, glu_b, proj_a, proj_b, w_out)


def _ffn_kernel(x1_ref, g_ref, wug_ref, wuv_ref, cw_all_ref, cb_all_ref, wd_ref,
                gfin_ref, initg_all_ref, initv_all_ref, y_ref, coutg_ref, coutv_ref, *rest,
                segs, chain_tiles, n_f, n_items, emit_bf16):
    w16_refs, rest = (rest[:3], rest[3:]) if emit_bf16 else ((None,) * 3, rest)
    (hn_ref, extg0_ref, extg1_ref, extv0_ref, extv1_ref, act0_ref, act1_ref,
     carryg_ref, carryv_ref) = rest
    ff_tile = wug_ref.shape[1]

    def weight_cols(w_ref, w16_ref, cols):
        if not emit_bf16:
            return w_ref[:, cols]
        w16 = w_ref[:, cols].astype(BF16)
        w16_ref[:, cols] = w16
        return w16

    t = pl.program_id(0)
    item1 = jnp.clip(t - 1, 0, n_items - 1)
    item2 = jnp.clip(t - 2, 0, n_items - 1)
    f1, tile1 = item1 % n_f, item1 // n_f
    f2 = item2 % n_f
    draining = t >= 2
    cwg_ref, cwv_ref = cw_all_ref.at[f1], cw_all_ref.at[n_f + f1]
    bg_ref, bv_ref = cb_all_ref.at[f1], cb_all_ref.at[n_f + f1]
    initg_ref, initv_ref = initg_all_ref.at[f1], initv_all_ref.at[f1]

    @pl.when(t == 0)
    def _():
        extg1_ref[...] = jnp.zeros(extg1_ref.shape, F32)
        extv1_ref[...] = jnp.zeros(extv1_ref.shape, F32)
        act0_ref[...] = jnp.zeros(act0_ref.shape, BF16)
        y_ref[...] = jnp.zeros(y_ref.shape, F32)

    @pl.when(jnp.minimum(t, n_items - 1) % n_f == 0)
    def _():
        hn_ref[...] = _rmsnorm(x1_ref[...], g_ref[...]).astype(BF16)

    @pl.when(jnp.logical_and(f2 == 0, draining))
    def _():
        y_ref[...] = x1_ref[...]

    _start_stream(initg_ref, carryg_ref.at[f1], tile1, chain_tiles)
    _start_stream(initv_ref, carryv_ref.at[f1], tile1, chain_tiles)

    bufs = ((extg0_ref, extv0_ref, act0_ref), (extg1_ref, extv1_ref, act1_ref))
    for par in (0, 1):
        @pl.when(t % 2 == par)
        def _(par=par):
            up_g, up_v, act_in = bufs[par]
            conv_src_g, conv_src_v, act_out = bufs[1 - par]

            def up_piece(dst_ref, w_ref, w16_ref, cols):
                return lambda: _stash_rows(
                    dst_ref, _dot(hn_ref[...], weight_cols(w_ref, w16_ref, cols)), segs, cols)

            def down_piece(cols):
                def run():
                    y_ref[:, cols] += _dot(act_in[...], weight_cols(wd_ref, w16_refs[2], cols))
                return run

            def gate_piece(k, s, rows, cols):
                def run():
                    base = s + CARRY_ROWS * k
                    g = _conv_piece(conv_src_g, cwg_ref, base, rows, cols) + bg_ref[:, cols]
                    v = _conv_piece(conv_src_v, cwv_ref, base, rows, cols) + bv_ref[:, cols]
                    act_out[s + rows[0]:s + rows[1], cols] = (jax.nn.silu(g) * v).astype(BF16)
                return run

            col_slices = lambda width, step: [slice(c, c + step) for c in range(0, width, step)]
            mxu = ([(up_piece(up_g, wug_ref, w16_refs[0], c), 4) for c in col_slices(ff_tile, MXU_COLS)]
                   + [(up_piece(up_v, wuv_ref, w16_refs[1], c), 4) for c in col_slices(ff_tile, MXU_COLS)]
                   + [(down_piece(c), 1) for c in col_slices(D_MODEL, MXU_COLS)])
            vpu = [gate_piece(k, s, (r, min(r + GATE_ROWS, n)), c)
                   for c in col_slices(ff_tile, 128)
                   for k, (s, n) in enumerate(segs) for r in range(0, n, GATE_ROWS)]

            _conv_prepare(conv_src_g, initg_ref, coutg_ref.at[:, f1], carryg_ref.at[f1], segs, chain_tiles)
            _conv_prepare(conv_src_v, initv_ref, coutv_ref.at[:, f1], carryv_ref.at[f1], segs, chain_tiles)
            total = sum(weight for _, weight in mxu)
            done, issued = 0, 0
            for run, weight in mxu:
                run()
                done += weight
                while issued < len(vpu) and issued * total < done * len(vpu):
                    vpu[issued]()
                    issued += 1

    @pl.when(jnp.logical_and(f2 == n_f - 1, draining))
    def _():
        y_ref[...] = _rmsnorm(y_ref[...], gfin_ref[...])


def _ffn(x1, g, w_up_g, w_up_v, conv_w, conv_b, w_down, g_fin, init_g, init_v,
         *, tm, ff_tile, segs, chain_tiles, emit_bf16):
    rows = x1.shape[0]
    n_tiles = rows // tm
    n_f = D_FF // ff_tile
    v0 = w_up_v.shape[1] // ff_tile - n_f
    n_items = n_tiles * n_f
    nseg = len(segs)
    n_out_seq = n_tiles // chain_tiles if chain_tiles else 1
    item = lambda t, lag: jnp.clip(t - lag, 0, n_items - 1)
    ff = lambda lag: (lambda t: item(t, lag) % n_f)
    tile = lambda lag: (lambda t: item(t, lag) // n_f)
    seq1 = lambda t: tile(1)(t) // chain_tiles if chain_tiles else 0
    cw_all = conv_w.reshape(3, 2 * n_f, ff_tile).transpose(1, 0, 2)
    cb_all = conv_b.reshape(2 * n_f, 1, ff_tile)
    per_tile = lambda a: a.reshape(a.shape[0], CARRY_ROWS, n_f, ff_tile).transpose(2, 0, 1, 3)
    init_g, init_v = per_tile(init_g), per_tile(init_v)
    cout_block = (nseg, n_f, CARRY_ROWS, ff_tile)
    cout_shape = jax.ShapeDtypeStruct((n_out_seq * nseg, n_f, CARRY_ROWS, ff_tile), F32)
    ext_shape = (tm + CARRY_ROWS * nseg, ff_tile)
    up_block = lambda first: pl.BlockSpec((D_MODEL, ff_tile), lambda t: (0, first + ff(0)(t)))
    down_block = pl.BlockSpec((ff_tile, D_MODEL), lambda t: (ff(2)(t), 0))
    out_specs = [
        pl.BlockSpec((tm, D_MODEL), lambda t: (tile(2)(t), 0)),
        pl.BlockSpec(cout_block, lambda t: (seq1(t), 0, 0, 0)),
        pl.BlockSpec(cout_block, lambda t: (seq1(t), 0, 0, 0)),
    ]
    out_shape = [jax.ShapeDtypeStruct((rows, D_MODEL), F32), cout_shape, cout_shape]
    if emit_bf16:
        out_specs += [up_block(0), up_block(0), down_block]
        out_shape += [jax.ShapeDtypeStruct((D_MODEL, D_FF), BF16)] * 2 + [jax.ShapeDtypeStruct((D_FF, D_MODEL), BF16)]
    y, cout_g, cout_v, *w16 = pl.pallas_call(
        functools.partial(_ffn_kernel, segs=segs, chain_tiles=chain_tiles, n_f=n_f, n_items=n_items,
                          emit_bf16=emit_bf16),
        grid=(n_items + 2,),
        in_specs=[
            pl.BlockSpec((tm, D_MODEL), lambda t: (tile(0)(t), 0)),
            pl.BlockSpec((1, D_MODEL), lambda t: (0, 0)),
            up_block(0),
            up_block(v0),
            _resident(cw_all.shape),
            _resident(cb_all.shape),
            down_block,
            pl.BlockSpec((1, D_MODEL), lambda t: (0, 0)),
            _resident(init_g.shape),
            _resident(init_v.shape),
        ],
        out_specs=out_specs,
        out_shape=out_shape,
        scratch_shapes=[
            pltpu.VMEM((tm, D_MODEL), BF16),
            pltpu.VMEM(ext_shape, F32), pltpu.VMEM(ext_shape, F32),
            pltpu.VMEM(ext_shape, F32), pltpu.VMEM(ext_shape, F32),
            pltpu.VMEM((tm, ff_tile), BF16), pltpu.VMEM((tm, ff_tile), BF16),
            pltpu.VMEM((n_f, CARRY_ROWS, ff_tile), F32),
            pltpu.VMEM((n_f, CARRY_ROWS, ff_tile), F32),
        ],
        compiler_params=_params(1),
        name="ffn",
    )(x1, g, w_up_g, w_up_v, cw_all, cb_all, w_down, g_fin, init_g, init_v)
    widen = lambda c: c.transpose(0, 2, 1, 3).reshape(c.shape[0], CARRY_ROWS, D_FF)
    return (y, widen(cout_g), widen(cout_v), *w16)


def _history_block(buf):
    return jnp.pad(buf, ((0, 0), (CARRY_ROWS - buf.shape[1], 0), (0, 0)))


def kernel(x_prompt, x_sample, cache_conv_a, state_ssm_re, state_ssm_im, cache_ffn_conv,
           meta_tokens, norm_mix_g, w_in, conv_a_w, ssm_lambda_re, ssm_lambda_im, ssm_log_dt,
           ssm_b_re, ssm_b_im, ssm_c_re, ssm_c_im, ssm_d, glu_w, glu_b, proj_a, proj_b,
           w_out, norm_ffn_g, w_up, ffn_conv_w, ffn_conv_b, w_down, norm_final_g):
    batch, seq, _ = x_prompt.shape
    dec_batch, dec_seq, _ = x_sample.shape
    assert dec_seq == CHUNK and seq % CHUNK == 0 and seq % ROW_TILE == 0 and seq % FFN_ROW_TILE == 0
    n_chunks = seq // CHUNK
    assert batch * n_chunks == 256 and dec_batch == 8

    w_a = w_in[0, :, :3 * W_A].astype(BF16)
    w_ug = jnp.concatenate([w_in[0, :, 3 * W_A + W_B:], w_in[0, :, 3 * W_A:3 * W_A + W_B]],
                           axis=1).astype(BF16)
    glu_w16, proj_a16, proj_b16 = glu_w[0].astype(BF16), proj_a[0].astype(BF16), proj_b[0].astype(BF16)
    w_out16 = w_out[0].astype(BF16)
    g_mix, g_ffn, g_fin = norm_mix_g[0][None], norm_ffn_g[0][None], norm_final_g[None]
    glu_b2, ffn_b2 = glu_b[0][None], ffn_conv_b[0][None]

    n_s = dec_batch * dec_seq
    n_pad = CHUNK - N_META
    rows_s = n_s + CHUNK
    x_s = jnp.concatenate([x_sample.reshape(n_s, D_MODEL), jnp.zeros((n_pad, D_MODEL), F32), meta_tokens],
                          axis=0)
    segs_s = (tuple((b * dec_seq, dec_seq) for b in range(dec_batch))
              + ((n_s, n_pad), (n_s + n_pad, N_META)))
    meta_seg = dec_batch + 1
    x_p = x_prompt.reshape(batch * seq, D_MODEL)
    segs_p = ((0, ROW_TILE),)
    chain = seq // ROW_TILE

    zero_hist = lambda c: jnp.zeros((1, CARRY_ROWS, c), F32)
    init_a_s = jnp.concatenate([_history_block(cache_conv_a[0]), zero_hist(W_A), zero_hist(W_A)], axis=0)
    hn_s, outa_s, ca_s = _mixer_a(x_s, g_mix, w_a, conv_a_w[0], init_a_s,
                                  tm=rows_s, segs=segs_s, chain_tiles=0)
    hn_p, outa_p, ca_p = _mixer_a(x_p, g_mix, w_a, conv_a_w[0], ca_s[meta_seg:],
                                  tm=ROW_TILE, segs=segs_p, chain_tiles=chain)
    gates_s, u_s = _gates_u(hn_s, w_ug, tm=rows_s)
    gates_p, u_p = _gates_u(hn_p, w_ug, tm=ROW_TILE)

    dup = lambda v: jnp.concatenate([v, v], axis=-1)[:, None, :]
    lre, lim = ssm_lambda_re[0], ssm_lambda_im[0]
    ldt = jnp.broadcast_to(ssm_log_dt[0][:, None], (SSM_G, SSM_P))
    bt_re, bt_im = ssm_b_re[0].swapaxes(1, 2), ssm_b_im[0].swapaxes(1, 2)
    bta = jnp.concatenate([bt_re, bt_im], axis=-1)
    btb = jnp.concatenate([-bt_im, bt_re], axis=-1)
    ddiag = ssm_d[0][:, :, None] * jnp.eye(SSM_H, dtype=F32)
    h0 = jnp.concatenate([state_ssm_re[0], state_ssm_im[0]], axis=-1).swapaxes(0, 1)
    yb_p, yb_s, h_fin = _ssm(u_p, u_s, lre[:, :, None], lim[:, :, None], ldt[:, :, None],
                             dup(lre), dup(lim), dup(ldt), bta, btb,
                             ssm_c_re[0].swapaxes(1, 2), ssm_c_im[0].swapaxes(1, 2), ddiag, h0)

    x1_s = _merge(yb_s, outa_s, gates_s, x_s, glu_w16, glu_b2, proj_a16, proj_b16, w_out16, tm=rows_s)
    x1_p = _merge(yb_p, outa_p, gates_p, x_p, glu_w16, glu_b2, proj_a16, proj_b16, w_out16, tm=ROW_TILE)

    hist_f = _history_block(cache_ffn_conv[0])
    init_g_s = jnp.concatenate([hist_f[:, :, :D_FF], zero_hist(D_FF), zero_hist(D_FF)], axis=0)
    init_v_s = jnp.concatenate([hist_f[:, :, D_FF:], zero_hist(D_FF), zero_hist(D_FF)], axis=0)
    y_s, cg_s, cv_s, w_up_g16, w_up_v16, w_down16 = _ffn(
        x1_s, g_ffn, w_up[0], w_up[0], ffn_conv_w[0], ffn_b2, w_down[0], g_fin, init_g_s, init_v_s,
        tm=rows_s, ff_tile=FF_TILE_SHORT, segs=segs_s, chain_tiles=0, emit_bf16=True)
    y_p, cg_p, cv_p = _ffn(
        x1_p, g_ffn, w_up_g16, w_up_v16, ffn_conv_w[0], ffn_b2, w_down16, g_fin,
        cg_s[meta_seg:], cv_s[meta_seg:], tm=FFN_ROW_TILE, ff_tile=FF_TILE,
        segs=((0, FFN_ROW_TILE),), chain_tiles=seq // FFN_ROW_TILE, emit_bf16=False)

    hist = lambda c: c[:, CARRY_ROWS - 2:][None]
    ffn_hist = lambda cg, cv: hist(jnp.concatenate([cg, cv], axis=-1))
    state = lambda h: (h[:, :, :SSM_P].swapaxes(0, 1)[None], h[:, :, SSM_P:].swapaxes(0, 1)[None])
    p_re, p_im = state(h_fin[:, 0:batch])
    s_re, s_im = state(h_fin[:, 8:8 + dec_batch])
    return (y_p.reshape(batch, seq, D_MODEL), y_s[:n_s].reshape(dec_batch, dec_seq, D_MODEL),
            hist(ca_p), p_re, p_im, ffn_hist(cg_p, cv_p),
            hist(ca_s[:dec_batch]), s_re, s_im, ffn_hist(cg_s[:dec_batch], cv_s[:dec_batch]))
```
